```python
import math
import jax, jax.numpy as jnp
from jax import lax
import numpy as np

D_MODEL = 2048
BATCH = 2
SEQ = 4096
DEPTH = 2
DEC_BATCH = 128
DEC_SEQ = 1
PAST_LEN = 16384
PAGE_SIZE = 128

N_EVEN = (DEPTH + 1) // 2
N_ODD = DEPTH // 2
Q_BLOCK = 128
EPS = 1e-6
ROPE_BASE = 10000.0

MLA_HEADS = 8
MLA_D_NOPE = 128
MLA_D_ROPE = 64
MLA_D_QK = MLA_D_NOPE + MLA_D_ROPE
MLA_D_V = 128
MLA_D_LATENT = 256

DSA_HEADS = 8
DSA_KV_HEADS = 2
DSA_HEAD_DIM = 128
IDX_HEADS = 4
IDX_DIM = 64
DSA_MAX_TOPK = 256

S5_CHANNELS = 1024
S5_GROUP = 16
S5_GROUPS = S5_CHANNELS // S5_GROUP
S5_STATE = 64

FOX_HEADS = 8
FOX_KV_HEADS = 2
FOX_HEAD_DIM = 128

D_FF = 4 * D_MODEL

EVEN_SPLITS = (MLA_HEADS * MLA_D_QK, MLA_D_LATENT, MLA_D_ROPE,
               DSA_HEADS * DSA_HEAD_DIM, DSA_KV_HEADS * DSA_HEAD_DIM, DSA_KV_HEADS * DSA_HEAD_DIM,
               IDX_HEADS * IDX_DIM, IDX_DIM, IDX_HEADS)
EVEN_IN = (MLA_HEADS * MLA_D_QK + MLA_D_LATENT + MLA_D_ROPE + DSA_HEADS * DSA_HEAD_DIM
           + 2 * DSA_KV_HEADS * DSA_HEAD_DIM + IDX_HEADS * IDX_DIM + IDX_DIM + IDX_HEADS)
EVEN_OUT = MLA_HEADS * MLA_D_V + DSA_HEADS * DSA_HEAD_DIM
ODD_SPLITS = (S5_CHANNELS, FOX_HEADS * FOX_HEAD_DIM, FOX_KV_HEADS * FOX_HEAD_DIM,
              FOX_KV_HEADS * FOX_HEAD_DIM, FOX_HEADS)
ODD_IN = S5_CHANNELS + FOX_HEADS * FOX_HEAD_DIM + 2 * FOX_KV_HEADS * FOX_HEAD_DIM + FOX_HEADS
ODD_OUT = S5_CHANNELS + FOX_HEADS * FOX_HEAD_DIM

kernel_name = 'hybrid_mla_dsa_s5_fox_step'


def _split(z, sizes):
    return jnp.split(z, np.cumsum(sizes)[:-1].tolist(), axis=-1)


def rms_norm(x, g):
    xf = x.astype(jnp.float32)
    y = xf * lax.rsqrt(jnp.mean(xf * xf, axis=-1, keepdims=True) + EPS)
    return (y * g.astype(jnp.float32)).astype(x.dtype)


def rope(x, pos):
    half = x.shape[-1] // 2
    inv_freq = ROPE_BASE ** (-jnp.arange(half, dtype=jnp.float32) / half)
    ang = pos.astype(jnp.float32)[:, None, None] * inv_freq
    cos, sin = jnp.cos(ang), jnp.sin(ang)
    xf = x.astype(jnp.float32)
    x1, x2 = xf[..., :half], xf[..., half:]
    return jnp.concatenate([x1 * cos - x2 * sin, x1 * sin + x2 * cos], axis=-1).astype(x.dtype)


def gather_rows(rows, idx):
    return jax.vmap(lambda r, i: r[i])(rows, idx)


def causal_block_attention(q, k, v, scale, log_cum=None):
    B, S, H, dk = q.shape
    G = k.shape[2]
    R = H // G
    dv = v.shape[-1]
    kf = k.astype(jnp.float32)
    vf = v.astype(jnp.float32)
    kpos = jnp.arange(S)
    if log_cum is not None:
        c_all = log_cum.astype(jnp.float32).reshape(B, S, G, R).transpose(0, 2, 3, 1)

    def one_block(i):
        start = i * Q_BLOCK
        qi = lax.dynamic_slice_in_dim(q, start, Q_BLOCK, axis=1).astype(jnp.float32)
        qi = qi.reshape(B, Q_BLOCK, G, R, dk)
        s = jnp.einsum('bqgrd,bsgd->bgrqs', qi, kf) * scale
        if log_cum is not None:
            cq = lax.dynamic_slice_in_dim(c_all, start, Q_BLOCK, axis=3)
            s = s + (cq[..., :, None] - c_all[..., None, :])
        qpos = start + jnp.arange(Q_BLOCK)
        s = jnp.where(kpos[None, :] <= qpos[:, None], s, -jnp.inf)
        p = jax.nn.softmax(s, axis=-1)
        o = jnp.einsum('bgrqs,bsgd->bqgrd', p, vf)
        return o.reshape(B, Q_BLOCK, H, dv)

    out = lax.map(one_block, jnp.arange(S // Q_BLOCK))
    return out.transpose(1, 0, 2, 3, 4).reshape(B, S, H, dv).astype(q.dtype)


def sparse_attend(q, ksel, vsel, valid, scale):
    B, T, H, d = q.shape
    G = ksel.shape[3]
    R = H // G
    qf = q.astype(jnp.float32).reshape(B, T, G, R, d)
    s = jnp.einsum('btgrd,btkgd->btgrk', qf, ksel.astype(jnp.float32)) * scale
    s = jnp.where(valid[:, :, None, None, :], s, -jnp.inf)
    p = jax.nn.softmax(s, axis=-1)
    o = jnp.einsum('btgrk,btkgd->btgrd', p, vsel.astype(jnp.float32))
    return o.reshape(B, T, H, d).astype(q.dtype)


def indexer_scores(q_idx, w_idx, k_idx):
    r = jax.nn.relu(jnp.einsum('bthe,ble->bthl', q_idx.astype(jnp.float32), k_idx.astype(jnp.float32)))
    return jnp.einsum('bthl,bth->btl', r, w_idx.astype(jnp.float32))


def even_project(h, pos, w_in, g_q_mla, g_latent, g_krope, g_q_dsa, g_k_dsa):
    B, T, _ = h.shape
    z_q, z_lat, z_kr, z_qd, z_kd, z_vd, z_qi, z_ki, z_wi = _split(h @ w_in, EVEN_SPLITS)
    q_mla = rms_norm(z_q.reshape(B, T, MLA_HEADS, MLA_D_QK), g_q_mla)
    q_mla = jnp.concatenate([q_mla[..., :MLA_D_NOPE], rope(q_mla[..., MLA_D_NOPE:], pos)], axis=-1)
    latent = rms_norm(z_lat, g_latent)
    krope = rope(rms_norm(z_kr, g_krope)[:, :, None, :], pos)[:, :, 0, :]
    q_dsa = rms_norm(z_qd.reshape(B, T, DSA_HEADS, DSA_HEAD_DIM), g_q_dsa)
    k_dsa = rms_norm(z_kd.reshape(B, T, DSA_KV_HEADS, DSA_HEAD_DIM), g_k_dsa)
    v_dsa = z_vd.reshape(B, T, DSA_KV_HEADS, DSA_HEAD_DIM)
    q_idx = z_qi.reshape(B, T, IDX_HEADS, IDX_DIM)
    return q_mla, latent, krope, q_dsa, k_dsa, v_dsa, q_idx, z_ki, z_wi


def mla_prompt(q, latent, krope, w_uk, w_uv):
    B, S, H, _ = q.shape
    k_nope = jnp.einsum('bsc,chn->bshn', latent, w_uk)
    v = jnp.einsum('bsc,chv->bshv', latent, w_uv)
    k = jnp.concatenate([k_nope, jnp.broadcast_to(krope[:, :, None, :], (B, S, H, MLA_D_ROPE))], axis=-1)
    return causal_block_attention(q, k, v, MLA_D_QK ** -0.5)


def mla_sample(q, latent_new, krope_new, cache_latent, cache_krope, li, page_table, w_uk, w_uv):
    Bd, T = q.shape[:2]
    P = page_table.shape[1] * PAGE_SIZE
    f32 = jnp.float32
    scale = MLA_D_QK ** -0.5
    past_lat = cache_latent[li, page_table].reshape(Bd, P, MLA_D_LATENT).astype(f32)
    past_kr = cache_krope[li, page_table].reshape(Bd, P, MLA_D_ROPE).astype(f32)
    lat_new = latent_new.astype(f32)
    kr_new = krope_new.astype(f32)
    qf = q.astype(f32)
    q_lat = jnp.einsum('bthn,chn->bthc', qf[..., :MLA_D_NOPE], w_uk.astype(f32))
    q_rope = qf[..., MLA_D_NOPE:]
    s_past = (jnp.einsum('bthc,bpc->bhtp', q_lat, past_lat)
              + jnp.einsum('bthr,bpr->bhtp', q_rope, past_kr)) * scale
    s_new = (jnp.einsum('bthc,buc->bhtu', q_lat, lat_new)
             + jnp.einsum('bthr,bur->bhtu', q_rope, kr_new)) * scale
    causal = jnp.arange(T)[None, :] <= jnp.arange(T)[:, None]
    s_new = jnp.where(causal, s_new, -jnp.inf)
    p = jax.nn.softmax(jnp.concatenate([s_past, s_new], axis=-1), axis=-1)
    o_lat = (jnp.einsum('bhtp,bpc->bthc', p[..., :P], past_lat)
             + jnp.einsum('bhtu,buc->bthc', p[..., P:], lat_new))
    return jnp.einsum('bthc,chv->bthv', o_lat, w_uv.astype(f32)).astype(q.dtype)


def dsa_prompt(q, k, v, q_idx, k_idx, w_idx):
    B, S, H, d = q.shape
    topk = min(DSA_MAX_TOPK, S // 4)
    kpos = jnp.arange(S)

    def one_block(i):
        start = i * Q_BLOCK
        sl = lambda a: lax.dynamic_slice_in_dim(a, start, Q_BLOCK, axis=1)
        qpos = start + jnp.arange(Q_BLOCK)
        causal = kpos[None, :] <= qpos[:, None]
        score = jnp.where(causal, indexer_scores(sl(q_idx), sl(w_idx), k_idx), -jnp.inf)
        _, sel = lax.top_k(score, topk)
        valid = sel <= qpos[None, :, None]
        return sparse_attend(sl(q), gather_rows(k, sel), gather_rows(v, sel), valid, DSA_HEAD_DIM ** -0.5)

    out = lax.map(one_block, jnp.arange(S // Q_BLOCK))
    return out.transpose(1, 0, 2, 3, 4).reshape(B, S, H, d)


def dsa_sample(q, k_new, v_new, q_idx, k_idx_new, w_idx, cache_k, cache_v, cache_idx_k, li, page_table):
    Bd, T = q.shape[:2]
    P = page_table.shape[1] * PAGE_SIZE
    topk = min(DSA_MAX_TOPK, (P + T) // 4)
    past_kidx = cache_idx_k[li, page_table].reshape(Bd, P, IDX_DIM)
    s_past = indexer_scores(q_idx, w_idx, past_kidx)
    causal = jnp.arange(T)[None, :] <= jnp.arange(T)[:, None]
    s_new = jnp.where(causal, indexer_scores(q_idx, w_idx, k_idx_new), -jnp.inf)
    _, sel = lax.top_k(jnp.concatenate([s_past, s_new], axis=-1), topk)
    valid = sel <= (P + jnp.arange(T))[None, :, None]
    is_past = (sel < P)[..., None, None]
    ps = jnp.minimum(sel, P - 1)
    phys = page_table[jnp.arange(Bd)[:, None, None], ps // PAGE_SIZE]
    off = ps % PAGE_SIZE
    nidx = jnp.clip(sel - P, 0, T - 1)
    ksel = jnp.where(is_past, cache_k[li, phys, off], gather_rows(k_new, nidx))
    vsel = jnp.where(is_past, cache_v[li, phys, off], gather_rows(v_new, nidx))
    return sparse_attend(q, ksel, vsel, valid, DSA_HEAD_DIM ** -0.5)


def odd_project(h, w_in, g_q, g_k, b_f):
    B, T, _ = h.shape
    u, zq, zk, zv, zf = _split(h @ w_in, ODD_SPLITS)
    q = rms_norm(zq.reshape(B, T, FOX_HEADS, FOX_HEAD_DIM), g_q)
    k = rms_norm(zk.reshape(B, T, FOX_KV_HEADS, FOX_HEAD_DIM), g_k)
    v = zv.reshape(B, T, FOX_KV_HEADS, FOX_HEAD_DIM)
    log_f = jax.nn.log_sigmoid((zf + b_f).astype(jnp.float32))
    return u, q, k, v, log_f


def s5_mix(u, x0_re, x0_im, lam_re, lam_im, log_dt, b_re, b_im, c_re, c_im, d_skip, w_glu, b_glu):
    f32 = jnp.float32
    B, T, _ = u.shape
    lam = lax.complex(lam_re.astype(f32), lam_im.astype(f32))
    dt = jnp.exp(log_dt.astype(f32))[:, None]
    a_bar = jnp.exp(lam * dt)
    b_bar = ((a_bar - 1.0) / lam)[..., None] * lax.complex(b_re.astype(f32), b_im.astype(f32))
    cmat = lax.complex(c_re.astype(f32), c_im.astype(f32))
    ug = u.astype(f32).reshape(B, T, S5_GROUPS, S5_GROUP).astype(jnp.complex64)
    bu = jnp.einsum('gpc,btgc->btgp', b_bar, ug)
    x0 = lax.complex(x0_re.astype(f32), x0_im.astype(f32))
    bu = bu.at[:, 0].add(a_bar * x0)
    a = jnp.broadcast_to(a_bar, bu.shape)

    def combine(left, right):
        a_l, b_l = left
        a_r, b_r = right
        return a_l * a_r, a_r * b_l + b_r

    _, xs = lax.associative_scan(combine, (a, bu), axis=1)
    y = jnp.einsum('gcp,btgp->btgc', cmat, xs).real.reshape(B, T, S5_CHANNELS)
    y = y + d_skip.astype(f32) * u.astype(f32)
    h = jax.nn.gelu(y)
    out = h * jax.nn.sigmoid(h @ w_glu.astype(f32) + b_glu.astype(f32))
    x_last = xs[:, -1]
    return out.astype(u.dtype), x_last.real, x_last.imag


def fox_sample(q, k_new, v_new, logf_new, cache_k, cache_v, cache_logf, li, page_table):
    Bd, T, H, d = q.shape
    G = FOX_KV_HEADS
    R = H // G
    P = page_table.shape[1] * PAGE_SIZE
    f32 = jnp.float32
    scale = FOX_HEAD_DIM ** -0.5
    past_k = cache_k[li, page_table].reshape(Bd, P, G, d).astype(f32)
    past_v = cache_v[li, page_table].reshape(Bd, P, G, d).astype(f32)
    past_lf = cache_logf[li, page_table].reshape(Bd, P, H).astype(f32)
    suffix = lax.cumsum(past_lf, axis=1, reverse=True) - past_lf
    cum_new = lax.cumsum(logf_new.astype(f32), axis=1)
    to_grh = lambda a: a.reshape(Bd, a.shape[1], G, R).transpose(0, 2, 3, 1)
    sfx, cn = to_grh(suffix), to_grh(cum_new)
    qf = q.astype(f32).reshape(Bd, T, G, R, d)
    s_past = jnp.einsum('btgrd,bpgd->bgrtp', qf, past_k) * scale + cn[..., :, None] + sfx[..., None, :]
    s_new = jnp.einsum('btgrd,bugd->bgrtu', qf, k_new.astype(f32)) * scale + cn[..., :, None] - cn[..., None, :]
    causal = jnp.arange(T)[None, :] <= jnp.arange(T)[:, None]
    s_new = jnp.where(causal, s_new, -jnp.inf)
    p = jax.nn.softmax(jnp.concatenate([s_past, s_new], axis=-1), axis=-1)
    o = (jnp.einsum('bgrtp,bpgd->btgrd', p[..., :P], past_v)
         + jnp.einsum('bgrtu,bugd->btgrd', p[..., P:], v_new.astype(f32)))
    return o.reshape(Bd, T, H, d).astype(q.dtype)


def sq_relu_mlp(h, w_up, w_down):
    return jnp.square(jax.nn.relu(h @ w_up)) @ w_down


def setup_inputs(seed: int = 0) -> dict:
    key = jax.random.key(seed)
    ks = iter(jax.random.split(key, 64))
    f32 = jnp.float32
    n_pages = PAST_LEN // PAGE_SIZE
    n_pool = (DEC_BATCH * n_pages * 5) // 4

    def nrm(shape, scale=None):
        v = jax.random.normal(next(ks), shape, f32)
        return v if scale is None else v * scale

    def gain(shape):
        return 1.0 + 0.02 * jax.random.normal(next(ks), shape, f32)

    E, O, G = N_EVEN, N_ODD, S5_GROUPS
    pool = (n_pool, PAGE_SIZE)
    x_prompt = nrm((BATCH, SEQ, D_MODEL))
    x_sample = nrm((DEC_BATCH, DEC_SEQ, D_MODEL))
    cache_mla_latent = nrm((E,) + pool + (MLA_D_LATENT,))
    cache_mla_krope = nrm((E,) + pool + (MLA_D_ROPE,))
    cache_dsa_k = nrm((E,) + pool + (DSA_KV_HEADS, DSA_HEAD_DIM))
    cache_dsa_v = nrm((E,) + pool + (DSA_KV_HEADS, DSA_HEAD_DIM))
    cache_dsa_idx_k = nrm((E,) + pool + (IDX_DIM,))
    cache_fox_k = nrm((O,) + pool + (FOX_KV_HEADS, FOX_HEAD_DIM))
    cache_fox_v = nrm((O,) + pool + (FOX_KV_HEADS, FOX_HEAD_DIM))
    cache_fox_logf = jax.nn.log_sigmoid(nrm((O,) + pool + (FOX_HEADS,)) + 4.0)
    state_s5_re = nrm((O, DEC_BATCH, G, S5_STATE), 0.1)
    state_s5_im = nrm((O, DEC_BATCH, G, S5_STATE), 0.1)
    page_table = jax.random.permutation(next(ks), n_pool)[: DEC_BATCH * n_pages]
    page_table = page_table.reshape(DEC_BATCH, n_pages).astype(jnp.int32)
    return {
        'x_prompt': x_prompt,
        'x_sample': x_sample,
        'cache_mla_latent': cache_mla_latent,
        'cache_mla_krope': cache_mla_krope,
        'cache_dsa_k': cache_dsa_k,
        'cache_dsa_v': cache_dsa_v,
        'cache_dsa_idx_k': cache_dsa_idx_k,
        'cache_fox_k': cache_fox_k,
        'cache_fox_v': cache_fox_v,
        'cache_fox_logf': cache_fox_logf,
        'state_s5_re': state_s5_re,
        'state_s5_im': state_s5_im,
        'page_table': page_table,
        'ln_mix_even': gain((E, D_MODEL)),
        'w_in_even': nrm((E, D_MODEL, EVEN_IN), D_MODEL ** -0.5),
        'g_q_mla': gain((E, MLA_D_QK)),
        'g_latent': gain((E, MLA_D_LATENT)),
        'g_krope': gain((E, MLA_D_ROPE)),
        'w_uk': nrm((E, MLA_D_LATENT, MLA_HEADS, MLA_D_NOPE), MLA_D_LATENT ** -0.5),
        'w_uv': nrm((E, MLA_D_LATENT, MLA_HEADS, MLA_D_V), MLA_D_LATENT ** -0.5),
        'g_q_dsa': gain((E, DSA_HEAD_DIM)),
        'g_k_dsa': gain((E, DSA_HEAD_DIM)),
        'w_out_even': nrm((E, EVEN_OUT, D_MODEL), EVEN_OUT ** -0.5),
        'ln_mix_odd': gain((O, D_MODEL)),
        'w_in_odd': nrm((O, D_MODEL, ODD_IN), D_MODEL ** -0.5),
        's5_lam_re': -0.5 + 0.01 * nrm((O, G, S5_STATE)),
        's5_lam_im': math.pi * jnp.arange(S5_STATE, dtype=f32) + 0.01 * nrm((O, G, S5_STATE)),
        's5_log_dt': jax.random.uniform(next(ks), (O, G), f32, math.log(1e-3), math.log(1e-1)),
        's5_b_re': nrm((O, G, S5_STATE, S5_GROUP), (2 * S5_GROUP) ** -0.5),
        's5_b_im': nrm((O, G, S5_STATE, S5_GROUP), (2 * S5_GROUP) ** -0.5),
        's5_c_re': nrm((O, G, S5_GROUP, S5_STATE), 0.5),
        's5_c_im': nrm((O, G, S5_GROUP, S5_STATE), 0.5),
        's5_d': nrm((O, S5_CHANNELS)),
        's5_w_glu': nrm((O, S5_CHANNELS, S5_CHANNELS), S5_CHANNELS ** -0.5),
        's5_b_glu': nrm((O, S5_CHANNELS), 0.01),
        'g_q_fox': gain((O, FOX_HEAD_DIM)),
        'g_k_fox': gain((O, FOX_HEAD_DIM)),
        'fox_b_f': jax.random.uniform(next(ks), (O, FOX_HEADS), f32, 1.0, 5.0),
        'w_out_odd': nrm((O, ODD_OUT, D_MODEL), ODD_OUT ** -0.5),
        'ln_mlp': gain((DEPTH, D_MODEL)),
        'w_up': nrm((DEPTH, D_MODEL, D_FF), D_MODEL ** -0.5),
        'w_down': nrm((DEPTH, D_FF, D_MODEL), D_FF ** -0.5),
    }


def reference(x_prompt, x_sample,
              cache_mla_latent, cache_mla_krope, cache_dsa_k, cache_dsa_v, cache_dsa_idx_k,
              cache_fox_k, cache_fox_v, cache_fox_logf, state_s5_re, state_s5_im,
              page_table,
              ln_mix_even, w_in_even, g_q_mla, g_latent, g_krope, w_uk, w_uv, g_q_dsa, g_k_dsa, w_out_even,
              ln_mix_odd, w_in_odd, s5_lam_re, s5_lam_im, s5_log_dt, s5_b_re, s5_b_im, s5_c_re, s5_c_im,
              s5_d, s5_w_glu, s5_b_glu, g_q_fox, g_k_fox, fox_b_f, w_out_odd,
              ln_mlp, w_up, w_down):
    B, S, _ = x_prompt.shape
    Bd, T, _ = x_sample.shape
    P = page_table.shape[1] * PAGE_SIZE
    pos_p = jnp.arange(S)
    pos_s = P + jnp.arange(T)
    yp, ys = x_prompt, x_sample
    rows = {name: [] for name in ('lat_p', 'lat_s', 'kr_p', 'kr_s', 'dk_p', 'dk_s', 'dv_p', 'dv_s',
                                  'di_p', 'di_s', 'fk_p', 'fk_s', 'fv_p', 'fv_s', 'fl_p', 'fl_s',
                                  'sre_p', 'sre_s', 'sim_p', 'sim_s')}
    for layer in range(DEPTH):
        li = layer // 2
        if layer % 2 == 0:
            qm, lat, kr, qd, kd, vd, qi, ki, wi = even_project(
                rms_norm(yp, ln_mix_even[li]), pos_p, w_in_even[li], g_q_mla[li], g_latent[li],
                g_krope[li], g_q_dsa[li], g_k_dsa[li])
            o_mla = mla_prompt(qm, lat, kr, w_uk[li], w_uv[li])
            o_dsa = dsa_prompt(qd, kd, vd, qi, ki, wi)
            yp = yp + jnp.concatenate([o_mla.reshape(B, S, -1), o_dsa.reshape(B, S, -1)], axis=-1) @ w_out_even[li]
            rows['lat_p'].append(lat); rows['kr_p'].append(kr); rows['dk_p'].append(kd)
            rows['dv_p'].append(vd); rows['di_p'].append(ki)
            qm, lat, kr, qd, kd, vd, qi, ki, wi = even_project(
                rms_norm(ys, ln_mix_even[li]), pos_s, w_in_even[li], g_q_mla[li], g_latent[li],
                g_krope[li], g_q_dsa[li], g_k_dsa[li])
            o_mla = mla_sample(qm, lat, kr, cache_mla_latent, cache_mla_krope, li, page_table, w_uk[li], w_uv[li])
            o_dsa = dsa_sample(qd, kd, vd, qi, ki, wi, cache_dsa_k, cache_dsa_v, cache_dsa_idx_k, li, page_table)
            ys = ys + jnp.concatenate([o_mla.reshape(Bd, T, -1), o_dsa.reshape(Bd, T, -1)], axis=-1) @ w_out_even[li]
            rows['lat_s'].append(lat); rows['kr_s'].append(kr); rows['dk_s'].append(kd)
            rows['dv_s'].append(vd); rows['di_s'].append(ki)
        else:
            s5_par = (s5_lam_re[li], s5_lam_im[li], s5_log_dt[li], s5_b_re[li], s5_b_im[li],
                      s5_c_re[li], s5_c_im[li], s5_d[li], s5_w_glu[li], s5_b_glu[li])
            u, q, k, v, lf = odd_project(rms_norm(yp, ln_mix_odd[li]), w_in_odd[li], g_q_fox[li], g_k_fox[li], fox_b_f[li])
            zero = jnp.zeros((B, S5_GROUPS, S5_STATE), jnp.float32)
            o_s5, sre, sim = s5_mix(u, zero, zero, *s5_par)
            o_fox = causal_block_attention(q, k, v, FOX_HEAD_DIM ** -0.5, log_cum=lax.cumsum(lf, axis=1))
            yp = yp + jnp.concatenate([o_s5, o_fox.reshape(B, S, -1)], axis=-1) @ w_out_odd[li]
            rows['fk_p'].append(k); rows['fv_p'].append(v); rows['fl_p'].append(lf)
            rows['sre_p'].append(sre); rows['sim_p'].append(sim)
            u, q, k, v, lf = odd_project(rms_norm(ys, ln_mix_odd[li]), w_in_odd[li], g_q_fox[li], g_k_fox[li], fox_b_f[li])
            o_s5, sre, sim = s5_mix(u, state_s5_re[li], state_s5_im[li], *s5_par)
            o_fox = fox_sample(q, k, v, lf, cache_fox_k, cache_fox_v, cache_fox_logf, li, page_table)
            ys = ys + jnp.concatenate([o_s5, o_fox.reshape(Bd, T, -1)], axis=-1) @ w_out_odd[li]
            rows['fk_s'].append(k); rows['fv_s'].append(v); rows['fl_s'].append(lf)
            rows['sre_s'].append(sre); rows['sim_s'].append(sim)
        yp = yp + sq_relu_mlp(rms_norm(yp, ln_mlp[layer]), w_up[layer], w_down[layer])
        ys = ys + sq_relu_mlp(rms_norm(ys, ln_mlp[layer]), w_up[layer], w_down[layer])
    return (yp, ys,
            jnp.stack(rows['lat_p']), jnp.stack(rows['lat_s']),
            jnp.stack(rows['kr_p']), jnp.stack(rows['kr_s']),
            jnp.stack(rows['dk_p']), jnp.stack(rows['dk_s']),
            jnp.stack(rows['dv_p']), jnp.stack(rows['dv_s']),
            jnp.stack(rows['di_p']), jnp.stack(rows['di_s']),
            jnp.stack(rows['fk_p']), jnp.stack(rows['fk_s']),
            jnp.stack(rows['fv_p']), jnp.stack(rows['fv_s']),
            jnp.stack(rows['fl_p']), jnp.stack(rows['fl_s']),
            jnp.stack(rows['sre_p']), jnp.stack(rows['sre_s']),
            jnp.stack(rows['sim_p']), jnp.stack(rows['sim_s']))
```

```python
import functools
import math

import numpy as np
import jax
import jax.numpy as jnp
from jax import lax
from jax.experimental import pallas as pl
from jax.experimental.pallas import tpu as pltpu

F32 = jnp.float32
BF16 = jnp.bfloat16
I32 = jnp.int32

LANES = 128
VMEM_LIMIT_BYTES = 56 * 1024 * 1024

D_MODEL = 2048
PAGE = 128
EPS = 1e-6
ROPE_BASE = 10000.0
MLA_H, MLA_NOPE, MLA_ROPE, MLA_V, MLA_LAT = 8, 128, 64, 128, 256
MLA_QK = MLA_NOPE + MLA_ROPE
MLA_PAD = 256
DSA_H, DSA_G, DSA_D = 8, 2, 128
IDX_H, IDX_D = 4, 64
DSA_TOPK = 256
S5_CH, S5_GRP, S5_G, S5_P = 1024, 16, 64, 64
S5_L = 16
FOX_H, FOX_G, FOX_D = 8, 2, 128
D_FF = 4 * D_MODEL
NEG = -1e30
INT_MIN = -(2 ** 31)

E_Q, E_LAT, E_QD, E_KD, E_VD, E_QI, E_KK, E_WI, E_N = 0, 2048, 2304, 3328, 3584, 3840, 4096, 4224, 4352
O_U, O_Q, O_K, O_V, O_F, O_N = 0, 1024, 2048, 2304, 2560, 2688


def _cp(sem):
    return pltpu.CompilerParams(dimension_semantics=sem, vmem_limit_bytes=VMEM_LIMIT_BYTES)


def _nt(a, b, precision=None):
    return lax.dot_general(a, b, (((1,), (1,)), ((), ())), preferred_element_type=F32, precision=precision)


def _rms_mm_kernel(x_ref, g_ref, w_ref, o_ref, h_ref):
    @pl.when(pl.program_id(1) == 0)
    def _():
        x = x_ref[...]
        ms = jnp.mean(x * x, axis=-1, keepdims=True)
        h_ref[...] = ((x * lax.rsqrt(ms + EPS)) * g_ref[...]).astype(BF16)

    o_ref[...] = jnp.dot(h_ref[...], w_ref[...], preferred_element_type=F32)


def rms_matmul(x, g, w, tm, tn):
    M, K = x.shape
    N = w.shape[1]
    return pl.pallas_call(
        _rms_mm_kernel, grid=(M // tm, N // tn),
        in_specs=[pl.BlockSpec((tm, K), lambda i, j: (i, 0)),
                  pl.BlockSpec((1, K), lambda i, j: (0, 0)),
                  pl.BlockSpec((K, tn), lambda i, j: (0, j))],
        out_specs=pl.BlockSpec((tm, tn), lambda i, j: (i, j)),
        out_shape=jax.ShapeDtypeStruct((M, N), F32),
        scratch_shapes=[pltpu.VMEM((tm, K), BF16)],
        compiler_params=_cp(("parallel", "arbitrary")), name="rms_matmul")(x, g, w)


def _mm_kernel(a_ref, w_ref, o_ref):
    o_ref[...] = jnp.dot(a_ref[...].astype(BF16), w_ref[...], preferred_element_type=F32).astype(o_ref.dtype)


def matmul(a, w, out_dtype, tm, tn):
    M, K = a.shape
    N = w.shape[1]
    return pl.pallas_call(
        _mm_kernel, grid=(M // tm, N // tn),
        in_specs=[pl.BlockSpec((tm, K), lambda i, j: (i, 0)), pl.BlockSpec((K, tn), lambda i, j: (0, j))],
        out_specs=pl.BlockSpec((tm, tn), lambda i, j: (i, j)),
        out_shape=jax.ShapeDtypeStruct((M, N), out_dtype),
        compiler_params=_cp(("parallel", "arbitrary")), name="matmul")(a, w)


def _out_proj_kernel(x_ref, a1_ref, a2_ref, w1_ref, w2_ref, o_ref):
    acc = jnp.dot(a1_ref[...].astype(BF16), w1_ref[...], preferred_element_type=F32)
    acc = acc + jnp.dot(a2_ref[...].astype(BF16), w2_ref[...], preferred_element_type=F32)
    o_ref[...] = x_ref[...] + acc


def out_proj(x, a1, a2, w1, w2, tm, tn):
    M, N = x.shape
    K1, K2 = a1.shape[1], a2.shape[1]
    return pl.pallas_call(
        _out_proj_kernel, grid=(M // tm, N // tn),
        in_specs=[pl.BlockSpec((tm, tn), lambda i, j: (i, j)),
                  pl.BlockSpec((tm, K1), lambda i, j: (i, 0)),
                  pl.BlockSpec((tm, K2), lambda i, j: (i, 0)),
                  pl.BlockSpec((K1, tn), lambda i, j: (0, j)),
                  pl.BlockSpec((K2, tn), lambda i, j: (0, j))],
        out_specs=pl.BlockSpec((tm, tn), lambda i, j: (i, j)),
        out_shape=jax.ShapeDtypeStruct((M, N), F32),
        compiler_params=_cp(("parallel", "arbitrary")), name="out_proj")(x, a1, a2, w1, w2)


def _mlp_kernel(x_ref, g_ref, wu_ref, wd_ref, o_ref, h_ref, acc_ref):
    f = pl.program_id(1)

    @pl.when(f == 0)
    def _():
        x = x_ref[...]
        ms = jnp.mean(x * x, axis=-1, keepdims=True)
        h_ref[...] = ((x * lax.rsqrt(ms + EPS)) * g_ref[...]).astype(BF16)
        acc_ref[...] = jnp.zeros_like(acc_ref)

    up = jnp.dot(h_ref[...], wu_ref[...], preferred_element_type=F32)
    act = jnp.square(jnp.maximum(up, 0.0)).astype(BF16)
    acc_ref[...] += jnp.dot(act, wd_ref[...], preferred_element_type=F32)

    @pl.when(f == pl.num_programs(1) - 1)
    def _():
        o_ref[...] = x_ref[...] + acc_ref[...]


def mlp(x, g, w_up, w_down, tm, tf):
    M, D = x.shape
    FF = w_up.shape[1]
    return pl.pallas_call(
        _mlp_kernel, grid=(M // tm, FF // tf),
        in_specs=[pl.BlockSpec((tm, D), lambda i, f: (i, 0)),
                  pl.BlockSpec((1, D), lambda i, f: (0, 0)),
                  pl.BlockSpec((D, tf), lambda i, f: (0, f)),
                  pl.BlockSpec((tf, D), lambda i, f: (f, 0))],
        out_specs=pl.BlockSpec((tm, D), lambda i, f: (i, 0)),
        out_shape=jax.ShapeDtypeStruct((M, D), F32),
        scratch_shapes=[pltpu.VMEM((tm, D), BF16), pltpu.VMEM((tm, D), F32)],
        compiler_params=_cp(("parallel", "arbitrary")), name="mlp")(x, g, w_up, w_down)


def _rope128(r, c, s):
    lane = lax.broadcasted_iota(I32, r.shape, 1)
    partner = jnp.where(lane < 32, pltpu.roll(r, 96, 1), pltpu.roll(r, 32, 1))
    return r * c + partner * s


def _even_post_kernel(z_ref, cos_ref, sin_ref, gq_ref, gl_ref, gkr_ref, gqd_ref, gkd_ref,
                      qm_ref, lat_ref, kr_ref, lk_ref, qd_ref, kd_ref, vd_ref, qi_ref, ki_ref, wi_ref):
    c = cos_ref[...]
    s = sin_ref[...]
    mla_scale = MLA_QK ** -0.5
    dsa_scale = DSA_D ** -0.5
    gq = gq_ref[...]
    for h in range(MLA_H):
        x = z_ref[:, E_Q + h * MLA_PAD:E_Q + (h + 1) * MLA_PAD]
        ms = jnp.sum(x * x, axis=-1, keepdims=True) * (1.0 / MLA_QK)
        y = (x * lax.rsqrt(ms + EPS)) * gq
        qm_ref[:, h * MLA_PAD:h * MLA_PAD + 128] = (y[:, :128] * mla_scale).astype(qm_ref.dtype)
        qm_ref[:, h * MLA_PAD + 128:(h + 1) * MLA_PAD] = (_rope128(y[:, 128:], c, s) * mla_scale).astype(qm_ref.dtype)
    x = z_ref[:, E_LAT:E_LAT + MLA_LAT]
    ms = jnp.mean(x * x, axis=-1, keepdims=True)
    lat = (x * lax.rsqrt(ms + EPS)) * gl_ref[...]
    lat_ref[...] = lat
    zk = z_ref[:, E_KK:E_KK + 128]
    lane = lax.broadcasted_iota(I32, zk.shape, 1)
    kr_in = jnp.where(lane < MLA_ROPE, zk, 0.0)
    ms = jnp.sum(kr_in * kr_in, axis=-1, keepdims=True) * (1.0 / MLA_ROPE)
    kr = _rope128((kr_in * lax.rsqrt(ms + EPS)) * gkr_ref[...], c, s)
    kr_ref[...] = kr[:, :MLA_ROPE]
    ki_ref[...] = zk[:, MLA_ROPE:]
    lk_ref[:, :MLA_LAT] = lat.astype(BF16)
    lk_ref[:, MLA_LAT:] = kr.astype(BF16)
    for h in range(DSA_H):
        x = z_ref[:, E_QD + h * DSA_D:E_QD + (h + 1) * DSA_D]
        ms = jnp.mean(x * x, axis=-1, keepdims=True)
        qd_ref[:, h * DSA_D:(h + 1) * DSA_D] = ((x * lax.rsqrt(ms + EPS)) * gqd_ref[...] * dsa_scale).astype(qd_ref.dtype)
    for h in range(DSA_G):
        x = z_ref[:, E_KD + h * DSA_D:E_KD + (h + 1) * DSA_D]
        ms = jnp.mean(x * x, axis=-1, keepdims=True)
        kd_ref[:, h * DSA_D:(h + 1) * DSA_D] = (x * lax.rsqrt(ms + EPS)) * gkd_ref[...]
    vd_ref[...] = z_ref[:, E_VD:E_VD + DSA_G * DSA_D]
    qi_ref[...] = z_ref[:, E_QI:E_QI + IDX_H * IDX_D].astype(qi_ref.dtype)
    wi_ref[...] = z_ref[:, E_WI:E_WI + 128]


def even_post(z, cos, sin, gq, gl, gkr, gqd, gkd, tm, q_dtype, n_pos_blocks):
    M = z.shape[0]
    row = lambda i: (i, 0)
    pos = lambda i: (i % n_pos_blocks, 0)
    fix = lambda i: (0, 0)
    widths = [(MLA_H * MLA_PAD, q_dtype), (MLA_LAT, F32), (MLA_ROPE, F32), (MLA_LAT + 128, BF16),
              (DSA_H * DSA_D, q_dtype), (DSA_G * DSA_D, F32), (DSA_G * DSA_D, F32),
              (IDX_H * IDX_D, q_dtype), (IDX_D, F32), (128, F32)]
    return pl.pallas_call(
        _even_post_kernel, grid=(M // tm,),
        in_specs=[pl.BlockSpec((tm, E_N), row), pl.BlockSpec((tm, 128), pos), pl.BlockSpec((tm, 128), pos),
                  pl.BlockSpec((1, MLA_PAD), fix), pl.BlockSpec((1, MLA_LAT), fix), pl.BlockSpec((1, 128), fix),
                  pl.BlockSpec((1, DSA_D), fix), pl.BlockSpec((1, DSA_D), fix)],
        out_specs=[pl.BlockSpec((tm, w), row) for w, _ in widths],
        out_shape=[jax.ShapeDtypeStruct((M, w), dt) for w, dt in widths],
        compiler_params=_cp(("parallel",)), name="even_post")(z, cos, sin, gq, gl, gkr, gqd, gkd)


def _log_sigmoid(x):
    return -(jnp.maximum(-x, 0.0) + jnp.log1p(jnp.exp(-jnp.abs(x))))


def _odd_post_kernel(z_ref, gq_ref, gk_ref, bf_ref, u_ref, q_ref, k_ref, v_ref, lf_ref, cum_ref, carry_ref, *, cumulative):
    scale = FOX_D ** -0.5
    u_ref[...] = z_ref[:, O_U:O_U + S5_CH]
    for h in range(FOX_H):
        x = z_ref[:, O_Q + h * FOX_D:O_Q + (h + 1) * FOX_D]
        ms = jnp.mean(x * x, axis=-1, keepdims=True)
        q_ref[:, h * FOX_D:(h + 1) * FOX_D] = ((x * lax.rsqrt(ms + EPS)) * gq_ref[...] * scale).astype(q_ref.dtype)
    for h in range(FOX_G):
        x = z_ref[:, O_K + h * FOX_D:O_K + (h + 1) * FOX_D]
        ms = jnp.mean(x * x, axis=-1, keepdims=True)
        k_ref[:, h * FOX_D:(h + 1) * FOX_D] = (x * lax.rsqrt(ms + EPS)) * gk_ref[...]
    v_ref[...] = z_ref[:, O_V:O_V + FOX_G * FOX_D]
    zf = z_ref[:, O_F:O_F + 128]
    lane = lax.broadcasted_iota(I32, zf.shape, 1)
    lf = jnp.where(lane < FOX_H, _log_sigmoid(zf + bf_ref[...]), 0.0)
    lf_ref[...] = lf
    if cumulative:
        tm = zf.shape[0]

        @pl.when(pl.program_id(1) == 0)
        def _():
            carry_ref[...] = jnp.zeros_like(carry_ref)

        r = lax.broadcasted_iota(I32, (tm, tm), 0)
        cidx = lax.broadcasted_iota(I32, (tm, tm), 1)
        tri = jnp.where(cidx <= r, 1.0, 0.0).astype(F32)
        cum = jnp.dot(tri, lf, preferred_element_type=F32, precision=lax.Precision.HIGHEST) + carry_ref[...]
        cum_ref[...] = cum
        carry_ref[...] = cum[tm - 1:tm, :]
    else:
        cum_ref[...] = lf


def odd_post(z, gq, gk, bf, tm, q_dtype, n_batch, cumulative):
    M = z.shape[0]
    nb = M // tm // n_batch
    row = lambda b, i: (b * nb + i, 0)
    fix = lambda b, i: (0, 0)
    widths = [(S5_CH, F32), (FOX_H * FOX_D, q_dtype), (FOX_G * FOX_D, F32), (FOX_G * FOX_D, F32), (128, F32), (128, F32)]
    return pl.pallas_call(
        functools.partial(_odd_post_kernel, cumulative=cumulative), grid=(n_batch, nb),
        in_specs=[pl.BlockSpec((tm, O_N), row), pl.BlockSpec((1, FOX_D), fix), pl.BlockSpec((1, FOX_D), fix),
                  pl.BlockSpec((1, 128), fix)],
        out_specs=[pl.BlockSpec((tm, w), row) for w, _ in widths],
        out_shape=[jax.ShapeDtypeStruct((M, w), dt) for w, dt in widths],
        scratch_shapes=[pltpu.VMEM((1, 128), F32)],
        compiler_params=_cp(("arbitrary", "arbitrary")), name="odd_post")(z, gq, gk, bf)


def _flash_kernel(*refs, R, dq, dv, T, has_bias, has_mask):
    q_ref, k_ref, v_ref = refs[:3]
    pos = 3
    cq_ref = ck_ref = mask_ref = None
    if has_bias:
        cq_ref, ck_ref = refs[pos:pos + 2]
        pos += 2
    if has_mask:
        mask_ref = refs[pos]
        pos += 1
    o_ref, m_ref, l_ref, acc_ref = refs[pos:pos + 4]
    qi = pl.program_id(2)
    ki = pl.program_id(3)

    @pl.when(ki == 0)
    def _():
        m_ref[...] = jnp.full_like(m_ref, NEG)
        l_ref[...] = jnp.zeros_like(l_ref)
        acc_ref[...] = jnp.zeros_like(acc_ref)

    def step(diag):
        k = k_ref[0].astype(BF16)
        v = v_ref[0].astype(BF16)
        for r in range(R):
            q = q_ref[0, :, r * dq:(r + 1) * dq]
            s = _nt(q, k)
            if has_bias:
                s = s + (cq_ref[0, 0, :, r:r + 1] - ck_ref[0, 0, r:r + 1, :])
            if has_mask:
                s = s + mask_ref[0].astype(F32)
            if diag:
                row = lax.broadcasted_iota(I32, (T, T), 0)
                col = lax.broadcasted_iota(I32, (T, T), 1)
                s = jnp.where(col <= row, s, NEG)
            m_prev = m_ref[r]
            m_new = jnp.maximum(m_prev, jnp.max(s, axis=-1, keepdims=True))
            alpha = jnp.exp(m_prev - m_new)
            p = jnp.exp(s - m_new)
            l_ref[r] = alpha * l_ref[r] + jnp.sum(p, axis=-1, keepdims=True)
            acc_ref[r] = alpha * acc_ref[r] + jnp.dot(p.astype(BF16), v, preferred_element_type=F32)
            m_ref[r] = m_new

    @pl.when(ki < qi)
    def _():
        step(False)

    @pl.when(ki == qi)
    def _():
        step(True)
        for r in range(R):
            o_ref[0, :, r * dv:(r + 1) * dv] = (acc_ref[r] / l_ref[r]).astype(o_ref.dtype)


def flash_attention(q, k, v, *, G, R, dq, dv, T, cq=None, ck=None, mask=None):
    B, S, _ = q.shape
    n = S // T
    kidx = lambda b, g, i, j: (b, jnp.minimum(i, j), g)
    in_specs = [pl.BlockSpec((1, T, R * dq), lambda b, g, i, j: (b, i, g)),
                pl.BlockSpec((1, T, dq), kidx), pl.BlockSpec((1, T, dv), kidx)]
    args = [q, k, v]
    if cq is not None:
        in_specs += [pl.BlockSpec((1, 1, T, 128), lambda b, g, i, j: (b, g, i, 0)),
                     pl.BlockSpec((1, 1, 8, T), lambda b, g, i, j: (b, g, 0, jnp.minimum(i, j)))]
        args += [cq, ck]
    if mask is not None:
        in_specs += [pl.BlockSpec((1, T, T), lambda b, g, i, j: (b, i, jnp.minimum(i, j)))]
        args += [mask]
    kern = functools.partial(_flash_kernel, R=R, dq=dq, dv=dv, T=T, has_bias=cq is not None, has_mask=mask is not None)
    return pl.pallas_call(
        kern, grid=(B, G, n, n), in_specs=in_specs,
        out_specs=pl.BlockSpec((1, T, R * dv), lambda b, g, i, j: (b, i, g)),
        out_shape=jax.ShapeDtypeStruct((B, S, G * R * dv), BF16),
        scratch_shapes=[pltpu.VMEM((R, T, 1), F32), pltpu.VMEM((R, T, 1), F32), pltpu.VMEM((R, T, dv), F32)],
        compiler_params=_cp(("parallel", "parallel", "parallel", "arbitrary")), name="flash_attention")(*args)


def _sortable_key(score):
    bits = lax.bitcast_convert_type(score, I32)
    key = bits ^ ((bits >> 31) & jnp.int32(0x7FFFFFFF))
    return jnp.where(key == -1, 0, key)


def _count(mask):
    return jnp.sum(jnp.where(mask, 1.0, 0.0), axis=1, keepdims=True)


def _topk_bias(key_ref, valid, k, j_ref, n_idx_bits):
    rows, cols = key_ref.shape
    kf = float(k)

    def kth(i, t):
        cand = t + jnp.left_shift(jnp.int32(1), 31 - i)
        cnt = _count(key_ref[...] >= cand)
        return jnp.where(cnt >= kf, cand, t)

    thr = lax.fori_loop(0, 32, kth, jnp.full((rows, 1), INT_MIN, I32))
    key = key_ref[...]
    gt = key > thr
    eq = (key == thr) & valid
    need = kf - _count(gt)
    n_eq = _count(eq)
    col = lax.broadcasted_iota(I32, (rows, cols), 1)
    j_ref[...] = jnp.full((rows, 1), cols, I32)
    overfull = jnp.max(jnp.where(n_eq > need, 1.0, 0.0))

    @pl.when(overfull > 0.0)
    def _():
        def cut(i, j0):
            cand = j0 + jnp.left_shift(jnp.int32(1), n_idx_bits - 1 - i)
            cnt = _count((key_ref[...] == thr) & valid & (col < cand))
            return jnp.where(cnt < need, cand, j0)

        j_ref[...] = lax.fori_loop(0, n_idx_bits, cut, jnp.zeros((rows, 1), I32))

    sel = valid & (gt | (eq & (col <= j_ref[...])))
    return jnp.where(sel, 0.0, NEG)


def _dsa_mask_kernel(qi_ref, wi_ref, ki_ref, bias_ref, key_ref, j_ref, *, tq, S):
    qb = pl.program_id(1)
    kb = ki_ref[0].astype(BF16)
    q = qi_ref[0]
    w = wi_ref[0]
    score = jnp.zeros((tq, S), F32)
    for h in range(IDX_H):
        d = _nt(q[:, h * IDX_D:(h + 1) * IDX_D], kb)
        score = score + w[:, h:h + 1] * jnp.maximum(d, 0.0)
    row = qb * tq + lax.broadcasted_iota(I32, (tq, S), 0)
    col = lax.broadcasted_iota(I32, (tq, S), 1)
    causal = col <= row
    key_ref[...] = jnp.where(causal, _sortable_key(score), INT_MIN)
    bias_ref[0] = _topk_bias(key_ref, causal, DSA_TOPK, j_ref, int(math.log2(S)) + 1).astype(BF16)


def dsa_prompt_mask(q_idx, w_idx, k_idx, tq):
    B, S, _ = q_idx.shape
    return pl.pallas_call(
        functools.partial(_dsa_mask_kernel, tq=tq, S=S), grid=(B, S // tq),
        in_specs=[pl.BlockSpec((1, tq, IDX_H * IDX_D), lambda b, i: (b, i, 0)),
                  pl.BlockSpec((1, tq, 128), lambda b, i: (b, i, 0)),
                  pl.BlockSpec((1, S, IDX_D), lambda b, i: (b, 0, 0))],
        out_specs=pl.BlockSpec((1, tq, S), lambda b, i: (b, i, 0)),
        out_shape=jax.ShapeDtypeStruct((B, S, S), BF16),
        scratch_shapes=[pltpu.VMEM((tq, S), I32), pltpu.VMEM((tq, 1), I32)],
        compiler_params=_cp(("parallel", "parallel")), name="dsa_prompt_mask")(q_idx, w_idx, k_idx)


def _s5_params(lam_re, lam_im, log_dt, b_re, b_im, c_re, c_im):
    dt = jnp.exp(log_dt)[:, None]
    def apow(n):
        mag = jnp.exp(lam_re * dt * n)
        return mag * jnp.cos(lam_im * dt * n), mag * jnp.sin(lam_im * dt * n)
    a_re, a_im = apow(1.0)
    den = lam_re * lam_re + lam_im * lam_im
    x, y = a_re - 1.0, a_im
    co_re = (x * lam_re + y * lam_im) / den
    co_im = (y * lam_re - x * lam_im) / den
    bb_re = co_re[..., None] * b_re - co_im[..., None] * b_im
    bb_im = co_re[..., None] * b_im + co_im[..., None] * b_re
    return a_re, a_im, bb_re, bb_im, apow


def _s5_chunk_operators(lam_re, lam_im, log_dt, b_re, b_im, c_re, c_im):
    a_re, a_im, bb_re, bb_im, apow = _s5_params(lam_re, lam_im, log_dt, b_re, b_im, c_re, c_im)
    L = S5_L
    taus = jnp.arange(L + 1, dtype=F32)
    pw = [apow(float(t)) for t in range(L + 1)]
    pw_re = jnp.stack([p[0] for p in pw])
    pw_im = jnp.stack([p[1] for p in pw])
    ab_re = pw_re[..., None] * bb_re[None] - pw_im[..., None] * bb_im[None]
    ab_im = pw_re[..., None] * bb_im[None] + pw_im[..., None] * bb_re[None]
    hp = lax.Precision.HIGHEST
    kern = (jnp.einsum('gcp,tgpd->tgcd', c_re, ab_re[:L], precision=hp)
            - jnp.einsum('gcp,tgpd->tgcd', c_im, ab_im[:L], precision=hp))
    t_out = np.arange(L)[None, :]
    s_in = np.arange(L)[:, None]
    lag = np.clip(t_out - s_in, 0, L - 1)
    m = kern[lag]
    m = jnp.where((t_out >= s_in)[:, :, None, None, None], m, 0.0)
    m = m.transpose(2, 0, 4, 1, 3).reshape(S5_G, L * S5_GRP, L * S5_GRP)
    w_re = ab_re[L - 1 - np.arange(L)]
    w_im = ab_im[L - 1 - np.arange(L)]
    w = jnp.concatenate([w_re, w_im], axis=2)
    w = w.transpose(1, 0, 3, 2).reshape(S5_G, L * S5_GRP, 2 * S5_P)
    ca_re = c_re[None] * pw_re[1:, :, None, :] - c_im[None] * pw_im[1:, :, None, :]
    ca_im = c_re[None] * pw_im[1:, :, None, :] + c_im[None] * pw_re[1:, :, None, :]
    v = jnp.concatenate([ca_re, -ca_im], axis=3)
    v = v.transpose(1, 3, 0, 2).reshape(S5_G, 2 * S5_P, L * S5_GRP)
    return m.astype(BF16), w.astype(BF16), v.astype(BF16), apow


def _s5_chunk_kernel(u_ref, m_ref, w_ref, v_ref, apr_ref, api_ref, y_ref, xr_ref, xi_ref, *, n_chunks, n_batch):
    u = u_ref[0].astype(BF16)
    y = jnp.dot(u, m_ref[0], preferred_element_type=F32)
    d = jnp.dot(u, w_ref[0], preferred_element_type=F32)
    xr = d[:, :S5_P]
    xi = d[:, S5_P:]
    rows = n_chunks * n_batch
    cidx = lax.broadcasted_iota(I32, (rows, S5_P), 0) % n_chunks
    for k in range(int(math.log2(n_chunks))):
        sh = 1 << k
        ar = apr_ref[0, k:k + 1, :]
        ai = api_ref[0, k:k + 1, :]
        keep = cidx >= sh
        sr = jnp.where(keep, pltpu.roll(xr, sh, 0), 0.0)
        si = jnp.where(keep, pltpu.roll(xi, sh, 0), 0.0)
        xr, xi = xr + ar * sr - ai * si, xi + ar * si + ai * sr
    keep = cidx >= 1
    xin = jnp.concatenate([jnp.where(keep, pltpu.roll(xr, 1, 0), 0.0),
                           jnp.where(keep, pltpu.roll(xi, 1, 0), 0.0)], axis=1).astype(BF16)
    y_ref[0] = y + jnp.dot(xin, v_ref[0], preferred_element_type=F32)
    xr_ref[0] = jnp.concatenate([xr[(b + 1) * n_chunks - 1:(b + 1) * n_chunks, :] for b in range(n_batch)], axis=0)
    xi_ref[0] = jnp.concatenate([xi[(b + 1) * n_chunks - 1:(b + 1) * n_chunks, :] for b in range(n_batch)], axis=0)


def s5_prompt(u, lam_re, lam_im, log_dt, b_re, b_im, c_re, c_im):
    B, T, _ = u.shape
    L = S5_L
    nch = T // L
    m, w, v, apow = _s5_chunk_operators(lam_re, lam_im, log_dt, b_re, b_im, c_re, c_im)
    nlev = int(math.log2(nch))
    ap = [apow(float(L * (1 << k))) for k in range(nlev)]
    apr = jnp.stack([p[0] for p in ap], axis=1)
    api = jnp.stack([p[1] for p in ap], axis=1)
    ug = u.reshape(B, nch, L, S5_G, S5_GRP).transpose(3, 0, 1, 2, 4).reshape(S5_G, B * nch, L * S5_GRP)
    rows = B * nch
    g3 = lambda g: (g, 0, 0)
    y, xr, xi = pl.pallas_call(
        functools.partial(_s5_chunk_kernel, n_chunks=nch, n_batch=B), grid=(S5_G,),
        in_specs=[pl.BlockSpec((1, rows, L * S5_GRP), g3), pl.BlockSpec((1, L * S5_GRP, L * S5_GRP), g3),
                  pl.BlockSpec((1, L * S5_GRP, 2 * S5_P), g3), pl.BlockSpec((1, 2 * S5_P, L * S5_GRP), g3),
                  pl.BlockSpec((1, nlev, S5_P), g3), pl.BlockSpec((1, nlev, S5_P), g3)],
        out_specs=[pl.BlockSpec((1, rows, L * S5_GRP), g3), pl.BlockSpec((1, B, S5_P), g3), pl.BlockSpec((1, B, S5_P), g3)],
        out_shape=[jax.ShapeDtypeStruct((S5_G, rows, L * S5_GRP), F32),
                   jax.ShapeDtypeStruct((S5_G, B, S5_P), F32), jax.ShapeDtypeStruct((S5_G, B, S5_P), F32)],
        compiler_params=_cp(("parallel",)), name="s5_chunk_scan")(ug, m, w, v, apr, api)
    y = y.reshape(S5_G, B, nch, L, S5_GRP).transpose(1, 2, 3, 0, 4).reshape(B, T, S5_CH)
    return y, xr.transpose(1, 0, 2), xi.transpose(1, 0, 2)


def _gelu_tanh(x):
    return 0.5 * x * (1.0 + jnp.tanh(math.sqrt(2.0 / math.pi) * (x + 0.044715 * (x * x * x))))


def _s5_glu_kernel(y_ref, u_ref, d_ref, w_ref, b_ref, o_ref):
    h = _gelu_tanh(y_ref[...] + d_ref[...] * u_ref[...])
    gate = jnp.dot(h.astype(BF16), w_ref[...], preferred_element_type=F32) + b_ref[...]
    o_ref[...] = (h * (1.0 / (1.0 + jnp.exp(-gate)))).astype(o_ref.dtype)


def s5_glu(y, u, d, w_glu, b_glu, tm):
    M, C = y.shape
    row = lambda i: (i, 0)
    fix = lambda i: (0, 0)
    return pl.pallas_call(
        _s5_glu_kernel, grid=(M // tm,),
        in_specs=[pl.BlockSpec((tm, C), row), pl.BlockSpec((tm, C), row), pl.BlockSpec((1, C), fix),
                  pl.BlockSpec((C, C), fix), pl.BlockSpec((1, C), fix)],
        out_specs=pl.BlockSpec((tm, C), row),
        out_shape=jax.ShapeDtypeStruct((M, C), BF16),
        compiler_params=_cp(("parallel",)), name="s5_glu")(y, u, d, w_glu, b_glu)


S5_GB = 8


def _s5_step_kernel(u_ref, x0r_ref, x0i_ref, ar_ref, ai_ref, br_ref, bi_ref, cr_ref, ci_ref, y_ref, xr_ref, xi_ref):
    u = u_ref[...].astype(BF16)
    ar, ai = ar_ref[...], ai_ref[...]
    x0r, x0i = x0r_ref[...], x0i_ref[...]
    xr = ar * x0r - ai * x0i + jnp.dot(u, br_ref[0], preferred_element_type=F32)
    xi = ar * x0i + ai * x0r + jnp.dot(u, bi_ref[0], preferred_element_type=F32)
    xr_ref[...] = xr
    xi_ref[...] = xi
    y_ref[...] = (jnp.dot(xr.astype(BF16), cr_ref[0], preferred_element_type=F32)
                  - jnp.dot(xi.astype(BF16), ci_ref[0], preferred_element_type=F32))


def s5_sample(u, x0_re, x0_im, lam_re, lam_im, log_dt, b_re, b_im, c_re, c_im):
    N = u.shape[0]
    a_re, a_im, bb_re, bb_im, _ = _s5_params(lam_re, lam_im, log_dt, b_re, b_im, c_re, c_im)
    nb = S5_G // S5_GB
    eye = jnp.eye(S5_GB, dtype=F32)
    bd = lambda t: jnp.einsum('jqpc,qr->jqcrp', t.reshape(nb, S5_GB, S5_P, S5_GRP), eye).reshape(
        nb, S5_GB * S5_GRP, S5_GB * S5_P).astype(BF16)
    cd = lambda t: jnp.einsum('jqcp,qr->jrpqc', t.reshape(nb, S5_GB, S5_GRP, S5_P), eye).reshape(
        nb, S5_GB * S5_P, S5_GB * S5_GRP).astype(BF16)
    wu, ws = S5_GB * S5_GRP, S5_GB * S5_P
    col = lambda j: (0, j)
    blk = lambda j: (j, 0, 0)
    y, xr, xi = pl.pallas_call(
        _s5_step_kernel, grid=(nb,),
        in_specs=[pl.BlockSpec((N, wu), col), pl.BlockSpec((N, ws), col), pl.BlockSpec((N, ws), col),
                  pl.BlockSpec((1, ws), col), pl.BlockSpec((1, ws), col),
                  pl.BlockSpec((1, wu, ws), blk), pl.BlockSpec((1, wu, ws), blk),
                  pl.BlockSpec((1, ws, wu), blk), pl.BlockSpec((1, ws, wu), blk)],
        out_specs=[pl.BlockSpec((N, wu), col), pl.BlockSpec((N, ws), col), pl.BlockSpec((N, ws), col)],
        out_shape=[jax.ShapeDtypeStruct((N, S5_CH), F32), jax.ShapeDtypeStruct((N, S5_G * S5_P), F32),
                   jax.ShapeDtypeStruct((N, S5_G * S5_P), F32)],
        compiler_params=_cp(("parallel",)), name="s5_step")(
            u, x0_re.reshape(N, -1), x0_im.reshape(N, -1), a_re.reshape(1, -1), a_im.reshape(1, -1),
            bd(bb_re), bd(bb_im), cd(c_re), cd(c_im))
    return y, xr.reshape(N, S5_G, S5_P), xi.reshape(N, S5_G, S5_P)


def _fetch(pt_ref, n, n_chunks, pages_per_chunk, slot, copies, sem, wait):
    b = n // n_chunks
    c = n % n_chunks

    def body(p, carry):
        page = 0 if wait else pt_ref[b, c * pages_per_chunk + p]
        for src, dst in copies:
            cp = pltpu.make_async_copy(src(page), dst(slot, p), sem.at[slot])
            if wait:
                cp.wait()
            else:
                cp.start()
        return carry

    lax.fori_loop(0, pages_per_chunk, body, 0)


def _pipeline_pages(pt_ref, n_chunks, pages_per_chunk, copies, sem):
    n = pl.program_id(0)
    slot = n % 2

    @pl.when(n == 0)
    def _():
        _fetch(pt_ref, n, n_chunks, pages_per_chunk, slot, copies, sem, False)

    @pl.when(n + 1 < pl.num_programs(0))
    def _():
        _fetch(pt_ref, n + 1, n_chunks, pages_per_chunk, 1 - slot, copies, sem, False)

    _fetch(pt_ref, n, n_chunks, pages_per_chunk, slot, copies, sem, True)
    return slot


def _softmax_update(s, v, m_ref, l_ref, acc_ref, idx):
    m_prev = m_ref[idx]
    m_new = jnp.maximum(m_prev, jnp.max(s, axis=-1, keepdims=True))
    alpha = jnp.exp(m_prev - m_new)
    p = jnp.exp(s - m_new)
    l_ref[idx] = alpha * l_ref[idx] + jnp.sum(p, axis=-1, keepdims=True)
    acc_ref[idx] = alpha * acc_ref[idx] + jnp.dot(p.astype(BF16), v, preferred_element_type=F32)
    m_ref[idx] = m_new


def _softmax_finish(s_new, v_new, m_ref, l_ref, acc_ref, idx):
    m_prev = m_ref[idx]
    m_new = jnp.maximum(m_prev, s_new)
    alpha = jnp.exp(m_prev - m_new)
    p = jnp.exp(s_new - m_new)
    return (alpha * acc_ref[idx] + p * v_new) / (alpha * l_ref[idx] + p)


def _init_softmax(c, m_ref, l_ref, acc_ref):
    @pl.when(c == 0)
    def _():
        m_ref[...] = jnp.full_like(m_ref, NEG)
        l_ref[...] = jnp.zeros_like(l_ref)
        acc_ref[...] = jnp.zeros_like(acc_ref)


def _mla_decode_kernel(pt_ref, ql_ref, qr_ref, ln_ref, kn_ref, clat, ckr, o_ref,
                       latbuf, krbuf, sem, m_ref, l_ref, acc_ref, *, n_chunks, ppc, li):
    copies = [(lambda pg: clat.at[li, pg], lambda s, p: latbuf.at[s, p]),
              (lambda pg: ckr.at[li, pg], lambda s, p: krbuf.at[s, p])]
    slot = _pipeline_pages(pt_ref, n_chunks, ppc, copies, sem)
    c = pl.program_id(0) % n_chunks
    _init_softmax(c, m_ref, l_ref, acc_ref)
    L = ppc * PAGE
    klat = latbuf[slot].reshape(L, MLA_LAT).astype(BF16)
    kkr = krbuf[slot].reshape(L, MLA_ROPE).astype(BF16)
    ql = ql_ref[0]
    qr = qr_ref[0]
    s = _nt(ql.astype(BF16), klat) + _nt(qr.astype(BF16), kkr)
    _softmax_update(s, klat, m_ref, l_ref, acc_ref, 0)

    @pl.when(c == n_chunks - 1)
    def _():
        ln = ln_ref[0]
        s_new = jnp.sum(ql * ln, axis=-1, keepdims=True) + jnp.sum(qr * kn_ref[0], axis=-1, keepdims=True)
        o_ref[0] = _softmax_finish(s_new, ln, m_ref, l_ref, acc_ref, 0)


def mla_decode(page_table, q_lat, q_rope, lat_new, kr_new, cache_lat, cache_kr, li, ppc):
    N, n_pages = page_table.shape
    n_chunks = n_pages // ppc
    row = lambda n, pt: (n // n_chunks, 0, 0)
    grid_spec = pltpu.PrefetchScalarGridSpec(
        num_scalar_prefetch=1, grid=(N * n_chunks,),
        in_specs=[pl.BlockSpec((1, MLA_H, MLA_LAT), row), pl.BlockSpec((1, MLA_H, MLA_ROPE), row),
                  pl.BlockSpec((1, 1, MLA_LAT), row), pl.BlockSpec((1, 1, MLA_ROPE), row),
                  pl.BlockSpec(memory_space=pl.ANY), pl.BlockSpec(memory_space=pl.ANY)],
        out_specs=pl.BlockSpec((1, MLA_H, MLA_LAT), row),
        scratch_shapes=[pltpu.VMEM((2, ppc, PAGE, MLA_LAT), F32), pltpu.VMEM((2, ppc, PAGE, MLA_ROPE), F32),
                        pltpu.SemaphoreType.DMA((2,)),
                        pltpu.VMEM((1, MLA_H, 1), F32), pltpu.VMEM((1, MLA_H, 1), F32), pltpu.VMEM((1, MLA_H, MLA_LAT), F32)])
    return pl.pallas_call(
        functools.partial(_mla_decode_kernel, n_chunks=n_chunks, ppc=ppc, li=li), grid_spec=grid_spec,
        out_shape=jax.ShapeDtypeStruct((N, MLA_H, MLA_LAT), F32),
        compiler_params=_cp(("arbitrary",)), name="mla_decode")(page_table, q_lat, q_rope, lat_new, kr_new, cache_lat, cache_kr)


def _gqa_decode_kernel(*refs, n_chunks, ppc, li, G, R, D, bias_mode):
    pt_ref, q_ref, kn_ref, vn_ref = refs[:4]
    if bias_mode == 'mask':
        bp_ref, bn_ref = refs[4:6]
    else:
        sfx_ref, cn_ref = refs[4:6]
    ck, cv, o_ref, kbuf, vbuf, sem, m_ref, l_ref, acc_ref = refs[6:]
    copies = []
    for g in range(G):
        copies.append((lambda pg, g=g: ck.at[li, pg, :, g, :], lambda s, p, g=g: kbuf.at[s, g, p]))
        copies.append((lambda pg, g=g: cv.at[li, pg, :, g, :], lambda s, p, g=g: vbuf.at[s, g, p]))
    slot = _pipeline_pages(pt_ref, n_chunks, ppc, copies, sem)
    c = pl.program_id(0) % n_chunks
    _init_softmax(c, m_ref, l_ref, acc_ref)
    L = ppc * PAGE
    q = q_ref[0]
    for g in range(G):
        k = kbuf[slot, g].reshape(L, D).astype(BF16)
        v = vbuf[slot, g].reshape(L, D).astype(BF16)
        s = _nt(q[g * R:(g + 1) * R].astype(BF16), k)
        if bias_mode == 'mask':
            s = s + bp_ref[0]
        else:
            sfx = jnp.concatenate([sfx_ref[0, p, g * R:(g + 1) * R, :] for p in range(ppc)], axis=1)
            s = s + (sfx + cn_ref[0, g * R:(g + 1) * R, :])
        _softmax_update(s, v, m_ref, l_ref, acc_ref, g)

    @pl.when(c == n_chunks - 1)
    def _():
        for g in range(G):
            qg = q[g * R:(g + 1) * R]
            s_new = jnp.sum(qg * kn_ref[0, g:g + 1, :], axis=-1, keepdims=True)
            if bias_mode == 'mask':
                s_new = s_new + bn_ref[0, :, 0:1]
            o_ref[0, g * R:(g + 1) * R, :] = _softmax_finish(s_new, vn_ref[0, g:g + 1, :], m_ref, l_ref, acc_ref, g)


def gqa_decode(page_table, q, k_new, v_new, bias_a, bias_b, cache_k, cache_v, li, ppc, bias_mode):
    N, n_pages = page_table.shape
    n_chunks = n_pages // ppc
    H, D = q.shape[1:]
    G = k_new.shape[1]
    R = H // G
    row = lambda n, pt: (n // n_chunks, 0, 0)
    if bias_mode == 'mask':
        bias_specs = [pl.BlockSpec((1, 1, ppc * PAGE), lambda n, pt: (n // n_chunks, 0, n % n_chunks)),
                      pl.BlockSpec((1, 1, 128), row)]
    else:
        bias_specs = [pl.BlockSpec((1, ppc, H, PAGE), lambda n, pt: (n // n_chunks, n % n_chunks, 0, 0)),
                      pl.BlockSpec((1, H, 1), row)]
    grid_spec = pltpu.PrefetchScalarGridSpec(
        num_scalar_prefetch=1, grid=(N * n_chunks,),
        in_specs=[pl.BlockSpec((1, H, D), row), pl.BlockSpec((1, G, D), row), pl.BlockSpec((1, G, D), row)]
        + bias_specs + [pl.BlockSpec(memory_space=pl.ANY), pl.BlockSpec(memory_space=pl.ANY)],
        out_specs=pl.BlockSpec((1, H, D), row),
        scratch_shapes=[pltpu.VMEM((2, G, ppc, PAGE, D), F32), pltpu.VMEM((2, G, ppc, PAGE, D), F32),
                        pltpu.SemaphoreType.DMA((2,)),
                        pltpu.VMEM((G, R, 1), F32), pltpu.VMEM((G, R, 1), F32), pltpu.VMEM((G, R, D), F32)])
    kern = functools.partial(_gqa_decode_kernel, n_chunks=n_chunks, ppc=ppc, li=li, G=G, R=R, D=D, bias_mode=bias_mode)
    return pl.pallas_call(
        kern, grid_spec=grid_spec, out_shape=jax.ShapeDtypeStruct((N, H, D), F32),
        compiler_params=_cp(("arbitrary",)), name="gqa_decode_" + bias_mode)(
            page_table, q, k_new, v_new, bias_a, bias_b, cache_k, cache_v)


def _dsa_score_kernel(pt_ref, qi_ref, wi_ref, cidx, o_ref, kbuf, sem, *, n_chunks, ppc, li):
    copies = [(lambda pg: cidx.at[li, pg], lambda s, p: kbuf.at[s, p])]
    slot = _pipeline_pages(pt_ref, n_chunks, ppc, copies, sem)
    L = ppc * PAGE
    k = kbuf[slot].reshape(L, IDX_D).astype(BF16)
    d = jnp.maximum(_nt(qi_ref[0].astype(BF16), k), 0.0)
    o_ref[0] = jnp.sum(wi_ref[0] * d, axis=0, keepdims=True)


def dsa_sample_scores(page_table, q_idx, w_idx, cache_idx, li, ppc):
    N, n_pages = page_table.shape
    n_chunks = n_pages // ppc
    row = lambda n, pt: (n // n_chunks, 0, 0)
    grid_spec = pltpu.PrefetchScalarGridSpec(
        num_scalar_prefetch=1, grid=(N * n_chunks,),
        in_specs=[pl.BlockSpec((1, IDX_H, IDX_D), row), pl.BlockSpec((1, IDX_H, 1), row), pl.BlockSpec(memory_space=pl.ANY)],
        out_specs=pl.BlockSpec((1, 1, ppc * PAGE), lambda n, pt: (n // n_chunks, 0, n % n_chunks)),
        scratch_shapes=[pltpu.VMEM((2, ppc, PAGE, IDX_D), F32), pltpu.SemaphoreType.DMA((2,))])
    return pl.pallas_call(
        functools.partial(_dsa_score_kernel, n_chunks=n_chunks, ppc=ppc, li=li), grid_spec=grid_spec,
        out_shape=jax.ShapeDtypeStruct((N, 1, n_pages * PAGE), F32),
        compiler_params=_cp(("arbitrary",)), name="dsa_sample_scores")(page_table, q_idx, w_idx, cache_idx)


def _dsa_select_kernel(sp_ref, qi_ref, wi_ref, kin_ref, bp_ref, bn_ref, key_ref, j_ref, *, P):
    rows = sp_ref.shape[0]
    q = qi_ref[...]
    kn = kin_ref[...]
    w = wi_ref[...]
    s_new = jnp.zeros((rows, 1), F32)
    for h in range(IDX_H):
        d = jnp.sum(q[:, h * IDX_D:(h + 1) * IDX_D].astype(BF16).astype(F32) * kn.astype(BF16).astype(F32),
                    axis=-1, keepdims=True)
        s_new = s_new + w[:, h:h + 1] * jnp.maximum(d, 0.0)
    lane = lax.broadcasted_iota(I32, (rows, 128), 1)
    key_ref[:, :P] = _sortable_key(sp_ref[...])
    key_ref[:, P:] = jnp.where(lane == 0, _sortable_key(jnp.broadcast_to(s_new, (rows, 128))), INT_MIN)
    col = lax.broadcasted_iota(I32, (rows, P + 128), 1)
    bias = _topk_bias(key_ref, col <= P, DSA_TOPK, j_ref, int(math.log2(P)) + 1)
    bp_ref[...] = bias[:, :P]
    bn_ref[...] = bias[:, P:]


def dsa_sample_select(scores_past, q_idx, w_idx, k_idx_new, tr):
    N, P = scores_past.shape
    row = lambda i: (i, 0)
    return pl.pallas_call(
        functools.partial(_dsa_select_kernel, P=P), grid=(N // tr,),
        in_specs=[pl.BlockSpec((tr, P), row), pl.BlockSpec((tr, IDX_H * IDX_D), row), pl.BlockSpec((tr, 128), row),
                  pl.BlockSpec((tr, IDX_D), row)],
        out_specs=[pl.BlockSpec((tr, P), row), pl.BlockSpec((tr, 128), row)],
        out_shape=[jax.ShapeDtypeStruct((N, P), F32), jax.ShapeDtypeStruct((N, 128), F32)],
        scratch_shapes=[pltpu.VMEM((tr, P + 128), I32), pltpu.VMEM((tr, 1), I32)],
        compiler_params=_cp(("parallel",)), name="dsa_sample_select")(scores_past, q_idx, w_idx, k_idx_new)


def _fox_suffix_kernel(pt_ref, clf, o_ref, buf, sem, *, n_pages, li):
    copies = [(lambda pg: clf.at[li, pg], lambda s, p: buf.at[s, p])]
    slot = _pipeline_pages(pt_ref, 1, n_pages, copies, sem)
    jj = lax.broadcasted_iota(I32, (PAGE, PAGE), 0)
    pp = lax.broadcasted_iota(I32, (PAGE, PAGE), 1)
    later = jnp.where(jj > pp, 1.0, 0.0).astype(F32)
    ones = jnp.ones((PAGE, PAGE), F32)
    hp = lax.Precision.HIGHEST
    tn = lambda a, b: lax.dot_general(a, b, (((0,), (0,)), ((), ())), preferred_element_type=F32, precision=hp)

    def body(i, carry):
        p = n_pages - 1 - i
        lf = buf[slot, p]
        o_ref[0, p] = tn(lf, later) + carry
        return carry + tn(lf, ones)

    lax.fori_loop(0, n_pages, body, jnp.zeros((FOX_H, PAGE), F32))


def fox_suffix(page_table, cache_logf, li):
    N, n_pages = page_table.shape
    grid_spec = pltpu.PrefetchScalarGridSpec(
        num_scalar_prefetch=1, grid=(N,),
        in_specs=[pl.BlockSpec(memory_space=pl.ANY)],
        out_specs=pl.BlockSpec((1, n_pages, FOX_H, PAGE), lambda n, pt: (n, 0, 0, 0)),
        scratch_shapes=[pltpu.VMEM((2, n_pages, PAGE, FOX_H), F32), pltpu.SemaphoreType.DMA((2,))])
    return pl.pallas_call(
        functools.partial(_fox_suffix_kernel, n_pages=n_pages, li=li), grid_spec=grid_spec,
        out_shape=jax.ShapeDtypeStruct((N, n_pages, FOX_H, PAGE), F32),
        compiler_params=_cp(("arbitrary",)), name="fox_suffix")(page_table, cache_logf)


def _per_head_kernel(a_ref, w_ref, o_ref, *, transpose_w):
    a = a_ref[...].astype(BF16)
    w = w_ref[...].astype(BF16)
    o_ref[...] = _nt(a, w) if transpose_w else jnp.dot(a, w, preferred_element_type=F32)


def per_head_matmul(a, w, d_in, d_out, transpose_w):
    N = a.shape[0]
    H = a.shape[1] // d_in
    wb = (d_out, d_in) if transpose_w else (d_in, d_out)
    return pl.pallas_call(
        functools.partial(_per_head_kernel, transpose_w=transpose_w), grid=(H,),
        in_specs=[pl.BlockSpec((N, d_in), lambda h: (0, h)), pl.BlockSpec(wb, lambda h: (0, h))],
        out_specs=pl.BlockSpec((N, d_out), lambda h: (0, h)),
        out_shape=jax.ShapeDtypeStruct((N, H * d_out), F32),
        compiler_params=_cp(("parallel",)), name="per_head_matmul")(a, w)


def _rope_tables(pos):
    half = MLA_ROPE // 2
    inv_freq = ROPE_BASE ** (-jnp.arange(half, dtype=F32) / half)
    ang = pos.astype(F32)[:, None] * inv_freq
    cos, sin = jnp.cos(ang), jnp.sin(ang)
    z = jnp.zeros((pos.shape[0], 128 - MLA_ROPE), F32)
    return jnp.concatenate([cos, cos, z], axis=1), jnp.concatenate([-sin, sin, z], axis=1)


def _pad_cols(a, n):
    return jnp.pad(a, ((0, 0), (0, n - a.shape[1])))


def _even_weights(w_in, g_q_mla, g_krope, w_uk, w_uv):
    D = w_in.shape[0]
    o = 0
    zq = w_in[:, o:o + MLA_H * MLA_QK].reshape(D, MLA_H, MLA_QK); o += MLA_H * MLA_QK
    zq = jnp.pad(zq, ((0, 0), (0, 0), (0, MLA_PAD - MLA_QK))).reshape(D, MLA_H * MLA_PAD)
    lat = w_in[:, o:o + MLA_LAT]; o += MLA_LAT
    kr = w_in[:, o:o + MLA_ROPE]; o += MLA_ROPE
    qd = w_in[:, o:o + DSA_H * DSA_D]; o += DSA_H * DSA_D
    kd = w_in[:, o:o + DSA_G * DSA_D]; o += DSA_G * DSA_D
    vd = w_in[:, o:o + DSA_G * DSA_D]; o += DSA_G * DSA_D
    qi = w_in[:, o:o + IDX_H * IDX_D]; o += IDX_H * IDX_D
    ki = w_in[:, o:o + IDX_D]; o += IDX_D
    wi = w_in[:, o:o + IDX_H]
    w = jnp.concatenate([zq, lat, qd, kd, vd, qi, kr, ki, _pad_cols(wi, 128)], axis=1).astype(BF16)
    gq = _pad_cols(g_q_mla[None, :], MLA_PAD)
    gkr = _pad_cols(g_krope[None, :], 128)
    wk = jnp.zeros((MLA_LAT + 128, MLA_H, MLA_PAD), F32)
    wk = wk.at[:MLA_LAT, :, :MLA_NOPE].set(w_uk)
    wk = wk.at[MLA_LAT:MLA_LAT + MLA_ROPE, :, MLA_NOPE:MLA_QK].set(
        jnp.broadcast_to(jnp.eye(MLA_ROPE, dtype=F32)[:, None, :], (MLA_ROPE, MLA_H, MLA_ROPE)))
    wv = jnp.zeros((MLA_LAT + 128, MLA_H * MLA_V), F32).at[:MLA_LAT].set(w_uv.reshape(MLA_LAT, MLA_H * MLA_V))
    wkv = jnp.concatenate([wk.reshape(MLA_LAT + 128, MLA_H * MLA_PAD), wv], axis=1).astype(BF16)
    return w, gq, gkr, wkv


def _odd_weights(w_in, fox_b_f):
    w = _pad_cols(w_in, O_N).astype(BF16)
    return w, _pad_cols(fox_b_f[None, :], 128)


def kernel(x_prompt, x_sample, cache_mla_latent, cache_mla_krope, cache_dsa_k, cache_dsa_v, cache_dsa_idx_k, cache_fox_k, cache_fox_v, cache_fox_logf, state_s5_re, state_s5_im, page_table, ln_mix_even, w_in_even, g_q_mla, g_latent, g_krope, w_uk, w_uv, g_q_dsa, g_k_dsa, w_out_even, ln_mix_odd, w_in_odd, s5_lam_re, s5_lam_im, s5_log_dt, s5_b_re, s5_b_im, s5_c_re, s5_c_im, s5_d, s5_w_glu, s5_b_glu, g_q_fox, g_k_fox, fox_b_f, w_out_odd, ln_mlp, w_up, w_down):
    B, S, D = x_prompt.shape
    N = x_sample.shape[0]
    n_pages = page_table.shape[1]
    P = n_pages * PAGE
    MP = B * S
    TM = 512
    TA = 512
    PPC = 32
    yp = x_prompt.reshape(MP, D)
    ys = x_sample.reshape(N, D)
    cos_p, sin_p = _rope_tables(jnp.arange(S))
    cos_s, sin_s = _rope_tables(jnp.full((N,), P, I32))
    outs = {}
    depth = ln_mlp.shape[0]
    for layer in range(depth):
        li = layer // 2
        if layer % 2 == 0:
            w, gq, gkr, wkv = _even_weights(w_in_even[li], g_q_mla[li], g_krope[li], w_uk[li], w_uv[li])
            gains = (gq, g_latent[li][None], gkr, g_q_dsa[li][None], g_k_dsa[li][None])
            w_out = w_out_even[li].astype(BF16)
            w_o1, w_o2 = w_out[:MLA_H * MLA_V], w_out[MLA_H * MLA_V:]
            z = rms_matmul(yp, ln_mix_even[li][None], w, TM, E_N // 2)
            qm, lat, kr, lk, qd, kd, vd, qi, ki, wi = even_post(z, cos_p, sin_p, *gains, 256, BF16, S // 256)
            kv = matmul(lk, wkv, BF16, TM, wkv.shape[1] // 3)
            k_mla, v_mla = kv[:, :MLA_H * MLA_PAD], kv[:, MLA_H * MLA_PAD:]
            r3 = lambda a: a.reshape(B, S, a.shape[1])
            o_mla = flash_attention(r3(qm), r3(k_mla), r3(v_mla), G=MLA_H, R=1, dq=MLA_PAD, dv=MLA_V, T=TA)
            mask = dsa_prompt_mask(r3(qi), r3(wi), r3(ki), 128)
            o_dsa = flash_attention(r3(qd), r3(kd), r3(vd), G=DSA_G, R=DSA_H // DSA_G, dq=DSA_D, dv=DSA_D, T=TA, mask=mask)
            yp = out_proj(yp, o_mla.reshape(MP, -1), o_dsa.reshape(MP, -1), w_o1, w_o2, TM, 1024)
            outs.setdefault('lat_p', []).append(lat.reshape(B, S, MLA_LAT))
            outs.setdefault('kr_p', []).append(kr.reshape(B, S, MLA_ROPE))
            outs.setdefault('dk_p', []).append(kd.reshape(B, S, DSA_G, DSA_D))
            outs.setdefault('dv_p', []).append(vd.reshape(B, S, DSA_G, DSA_D))
            outs.setdefault('di_p', []).append(ki.reshape(B, S, IDX_D))
            z = rms_matmul(ys, ln_mix_even[li][None], w, N, E_N // 2)
            qm, lat, kr, lk, qd, kd, vd, qi, ki, wi = even_post(z, cos_s, sin_s, *gains, N, F32, 1)
            qm3 = qm.reshape(N, MLA_H, MLA_PAD)
            q_nope = qm3[:, :, :MLA_NOPE].reshape(N, MLA_H * MLA_NOPE)
            q_rope = qm3[:, :, MLA_NOPE:MLA_QK]
            w_uk2 = w_uk[li].reshape(MLA_LAT, MLA_H * MLA_NOPE)
            q_lat = per_head_matmul(q_nope, w_uk2, MLA_NOPE, MLA_LAT, True).reshape(N, MLA_H, MLA_LAT)
            o_lat = mla_decode(page_table, q_lat, q_rope, lat[:, None, :], kr[:, None, :],
                               cache_mla_latent, cache_mla_krope, li, PPC)
            o_mla = per_head_matmul(o_lat.reshape(N, MLA_H * MLA_LAT), w_uv[li].reshape(MLA_LAT, MLA_H * MLA_V),
                                    MLA_LAT, MLA_V, False)
            sc = dsa_sample_scores(page_table, qi.reshape(N, IDX_H, IDX_D), wi[:, :IDX_H, None], cache_dsa_idx_k, li, PPC)
            bias_p, bias_n = dsa_sample_select(sc.reshape(N, P), qi, wi, ki, 32)
            o_dsa = gqa_decode(page_table, qd.reshape(N, DSA_H, DSA_D), kd.reshape(N, DSA_G, DSA_D),
                               vd.reshape(N, DSA_G, DSA_D), bias_p[:, None, :], bias_n[:, None, :],
                               cache_dsa_k, cache_dsa_v, li, PPC, 'mask')
            ys = out_proj(ys, o_mla, o_dsa.reshape(N, -1), w_o1, w_o2, N, 1024)
            outs.setdefault('lat_s', []).append(lat.reshape(N, 1, MLA_LAT))
            outs.setdefault('kr_s', []).append(kr.reshape(N, 1, MLA_ROPE))
            outs.setdefault('dk_s', []).append(kd.reshape(N, 1, DSA_G, DSA_D))
            outs.setdefault('dv_s', []).append(vd.reshape(N, 1, DSA_G, DSA_D))
            outs.setdefault('di_s', []).append(ki.reshape(N, 1, IDX_D))
        else:
            w, bf = _odd_weights(w_in_odd[li], fox_b_f[li])
            s5p = (s5_lam_re[li], s5_lam_im[li], s5_log_dt[li], s5_b_re[li], s5_b_im[li], s5_c_re[li], s5_c_im[li])
            w_glu = s5_w_glu[li].astype(BF16)
            w_out = w_out_odd[li].astype(BF16)
            w_o1, w_o2 = w_out[:S5_CH], w_out[S5_CH:]
            R = FOX_H // FOX_G
            z = rms_matmul(yp, ln_mix_odd[li][None], w, TM, O_N // 3)
            u, q, k, v, lf, cum = odd_post(z, g_q_fox[li][None], g_k_fox[li][None], bf, 256, BF16, B, True)
            y5, sre, sim = s5_prompt(u.reshape(B, S, S5_CH), *s5p)
            o_s5 = s5_glu(y5.reshape(MP, S5_CH), u, s5_d[li][None], w_glu, s5_b_glu[li][None], TM)
            cum3 = cum[:, :FOX_H].reshape(B, S, FOX_G, R)
            cq = jnp.pad(cum3.transpose(0, 2, 1, 3), ((0, 0), (0, 0), (0, 0), (0, 128 - R)))
            ck = jnp.pad(cum3.transpose(0, 2, 3, 1), ((0, 0), (0, 0), (0, 8 - R), (0, 0)))
            r3 = lambda a: a.reshape(B, S, a.shape[1])
            o_fox = flash_attention(r3(q), r3(k), r3(v), G=FOX_G, R=R, dq=FOX_D, dv=FOX_D, T=TA, cq=cq, ck=ck)
            yp = out_proj(yp, o_s5, o_fox.reshape(MP, -1), w_o1, w_o2, TM, 1024)
            outs.setdefault('fk_p', []).append(k.reshape(B, S, FOX_G, FOX_D))
            outs.setdefault('fv_p', []).append(v.reshape(B, S, FOX_G, FOX_D))
            outs.setdefault('fl_p', []).append(lf[:, :FOX_H].reshape(B, S, FOX_H))
            outs.setdefault('sre_p', []).append(sre)
            outs.setdefault('sim_p', []).append(sim)
            z = rms_matmul(ys, ln_mix_odd[li][None], w, N, O_N // 3)
            u, q, k, v, lf, _ = odd_post(z, g_q_fox[li][None], g_k_fox[li][None], bf, N, F32, 1, False)
            y5, sre, sim = s5_sample(u, state_s5_re[li], state_s5_im[li], *s5p)
            o_s5 = s5_glu(y5, u, s5_d[li][None], w_glu, s5_b_glu[li][None], N)
            sfx = fox_suffix(page_table, cache_fox_logf, li)
            o_fox = gqa_decode(page_table, q.reshape(N, FOX_H, FOX_D), k.reshape(N, FOX_G, FOX_D),
                               v.reshape(N, FOX_G, FOX_D), sfx, lf[:, :FOX_H, None],
                               cache_fox_k, cache_fox_v, li, PPC, 'forget')
            ys = out_proj(ys, o_s5, o_fox.reshape(N, -1), w_o1, w_o2, N, 1024)
            outs.setdefault('fk_s', []).append(k.reshape(N, 1, FOX_G, FOX_D))
            outs.setdefault('fv_s', []).append(v.reshape(N, 1, FOX_G, FOX_D))
            outs.setdefault('fl_s', []).append(lf[:, :FOX_H].reshape(N, 1, FOX_H))
            outs.setdefault('sre_s', []).append(sre)
            outs.setdefault('sim_s', []).append(sim)
        wu = w_up[layer].astype(BF16)
        wd = w_down[layer].astype(BF16)
        yp = mlp(yp, ln_mlp[layer][None], wu, wd, TM, 512)
        ys = mlp(ys, ln_mlp[layer][None], wu, wd, N, 512)
    order = ('lat_p', 'lat_s', 'kr_p', 'kr_s', 'dk_p', 'dk_s', 'dv_p', 'dv_s', 'di_p', 'di_s',
             'fk_p', 'fk_s', 'fv_p', 'fv_s', 'fl_p', 'fl_s', 'sre_p', 'sre_s', 'sim_p', 'sim_s')
    return (yp.reshape(B, S, D), ys.reshape(N, 1, D)) + tuple(jnp.stack(outs[name]) for name in order)
```

```python
import functools
import math

import numpy as np
import jax
import jax.numpy as jnp
from jax import lax
from jax.experimental import pallas as pl
from jax.experimental.pallas import tpu as pltpu

F32 = jnp.float32
BF16 = jnp.bfloat16
I32 = jnp.int32

LANES = 128
VMEM_LIMIT_BYTES = 56 * 1024 * 1024

D_MODEL = 2048
PAGE = 128
EPS = 1e-6
ROPE_BASE = 10000.0
MLA_H, MLA_NOPE, MLA_ROPE, MLA_V, MLA_LAT = 8, 128, 64, 128, 256
MLA_QK = MLA_NOPE + MLA_ROPE
MLA_PAD = 256
DSA_H, DSA_G, DSA_D = 8, 2, 128
IDX_H, IDX_D = 4, 64
DSA_TOPK = 256
S5_CH, S5_GRP, S5_G, S5_P = 1024, 16, 64, 64
S5_L = 16
FOX_H, FOX_G, FOX_D = 8, 2, 128
D_FF = 4 * D_MODEL
NEG = -1e30
INT_MIN = -(2 ** 31)
LOG2E = math.log2(math.e)

E_Q, E_LAT, E_QD, E_KD, E_VD, E_QI, E_KK, E_WI, E_N = 0, 2048, 2304, 3328, 3584, 3840, 4096, 4224, 4352
O_U, O_Q, O_K, O_V, O_F, O_N = 0, 1024, 2048, 2304, 2560, 2688


def _cp(sem):
    return pltpu.CompilerParams(dimension_semantics=sem, vmem_limit_bytes=VMEM_LIMIT_BYTES)


def _nt(a, b, precision=None):
    return lax.dot_general(a, b, (((1,), (1,)), ((), ())), preferred_element_type=F32, precision=precision)


def _rms_mm_kernel(x_ref, g_ref, w_ref, o_ref, h_ref):
    @pl.when(pl.program_id(1) == 0)
    def _():
        x = x_ref[...]
        ms = jnp.mean(x * x, axis=-1, keepdims=True)
        h_ref[...] = ((x * lax.rsqrt(ms + EPS)) * g_ref[...]).astype(BF16)

    o_ref[...] = jnp.dot(h_ref[...], w_ref[...], preferred_element_type=F32)


def rms_matmul(x, g, w, tm, tn):
    M, K = x.shape
    N = w.shape[1]
    return pl.pallas_call(
        _rms_mm_kernel, grid=(M // tm, N // tn),
        in_specs=[pl.BlockSpec((tm, K), lambda i, j: (i, 0)),
                  pl.BlockSpec((1, K), lambda i, j: (0, 0)),
                  pl.BlockSpec((K, tn), lambda i, j: (0, j))],
        out_specs=pl.BlockSpec((tm, tn), lambda i, j: (i, j)),
        out_shape=jax.ShapeDtypeStruct((M, N), F32),
        scratch_shapes=[pltpu.VMEM((tm, K), BF16)],
        compiler_params=_cp(("parallel", "arbitrary")), name="rms_matmul")(x, g, w)


def _mm_kernel(a_ref, w_ref, o_ref):
    o_ref[...] = jnp.dot(a_ref[...].astype(BF16), w_ref[...], preferred_element_type=F32).astype(o_ref.dtype)


def matmul(a, w, out_dtype, tm, tn):
    M, K = a.shape
    N = w.shape[1]
    return pl.pallas_call(
        _mm_kernel, grid=(M // tm, N // tn),
        in_specs=[pl.BlockSpec((tm, K), lambda i, j: (i, 0)), pl.BlockSpec((K, tn), lambda i, j: (0, j))],
        out_specs=pl.BlockSpec((tm, tn), lambda i, j: (i, j)),
        out_shape=jax.ShapeDtypeStruct((M, N), out_dtype),
        compiler_params=_cp(("parallel", "arbitrary")), name="matmul")(a, w)


def _out_proj_kernel(x_ref, a1_ref, a2_ref, w1_ref, w2_ref, o_ref):
    acc = jnp.dot(a1_ref[...].astype(BF16), w1_ref[...], preferred_element_type=F32)
    acc = acc + jnp.dot(a2_ref[...].astype(BF16), w2_ref[...], preferred_element_type=F32)
    o_ref[...] = x_ref[...] + acc


def out_proj(x, a1, a2, w1, w2, tm, tn):
    M, N = x.shape
    K1, K2 = a1.shape[1], a2.shape[1]
    return pl.pallas_call(
        _out_proj_kernel, grid=(M // tm, N // tn),
        in_specs=[pl.BlockSpec((tm, tn), lambda i, j: (i, j)),
                  pl.BlockSpec((tm, K1), lambda i, j: (i, 0)),
                  pl.BlockSpec((tm, K2), lambda i, j: (i, 0)),
                  pl.BlockSpec((K1, tn), lambda i, j: (0, j)),
                  pl.BlockSpec((K2, tn), lambda i, j: (0, j))],
        out_specs=pl.BlockSpec((tm, tn), lambda i, j: (i, j)),
        out_shape=jax.ShapeDtypeStruct((M, N), F32),
        compiler_params=_cp(("parallel", "arbitrary")), name="out_proj")(x, a1, a2, w1, w2)


def _mlp_kernel(x_ref, g_ref, wu_ref, wd_ref, o_ref, h_ref, acc_ref):
    f = pl.program_id(1)

    @pl.when(f == 0)
    def _():
        x = x_ref[...]
        ms = jnp.mean(x * x, axis=-1, keepdims=True)
        h_ref[...] = ((x * lax.rsqrt(ms + EPS)) * g_ref[...]).astype(BF16)
        acc_ref[...] = jnp.zeros_like(acc_ref)

    up = jnp.dot(h_ref[...], wu_ref[...], preferred_element_type=F32)
    act = jnp.square(jnp.maximum(up, 0.0)).astype(BF16)
    acc_ref[...] += jnp.dot(act, wd_ref[...], preferred_element_type=F32)

    @pl.when(f == pl.num_programs(1) - 1)
    def _():
        o_ref[...] = x_ref[...] + acc_ref[...]


def mlp(x, g, w_up, w_down, tm, tf):
    M, D = x.shape
    FF = w_up.shape[1]
    return pl.pallas_call(
        _mlp_kernel, grid=(M // tm, FF // tf),
        in_specs=[pl.BlockSpec((tm, D), lambda i, f: (i, 0)),
                  pl.BlockSpec((1, D), lambda i, f: (0, 0)),
                  pl.BlockSpec((D, tf), lambda i, f: (0, f)),
                  pl.BlockSpec((tf, D), lambda i, f: (f, 0))],
        out_specs=pl.BlockSpec((tm, D), lambda i, f: (i, 0)),
        out_shape=jax.ShapeDtypeStruct((M, D), F32),
        scratch_shapes=[pltpu.VMEM((tm, D), BF16), pltpu.VMEM((tm, D), F32)],
        compiler_params=_cp(("parallel", "arbitrary")), name="mlp")(x, g, w_up, w_down)


def _rope128(r, c, s):
    lane = lax.broadcasted_iota(I32, r.shape, 1)
    partner = jnp.where(lane < 32, pltpu.roll(r, 96, 1), pltpu.roll(r, 32, 1))
    return r * c + partner * s


def _even_post_kernel(z_ref, cos_ref, sin_ref, gq_ref, gl_ref, gkr_ref, gqd_ref, gkd_ref,
                      qm_ref, lat_ref, kr_ref, lk_ref, qd_ref, kd_ref, vd_ref, qi_ref, ki_ref, wi_ref):
    c = cos_ref[...]
    s = sin_ref[...]
    mla_scale = MLA_QK ** -0.5 * LOG2E
    dsa_scale = DSA_D ** -0.5 * LOG2E
    gq = gq_ref[...]
    for h in range(MLA_H):
        x = z_ref[:, E_Q + h * MLA_PAD:E_Q + (h + 1) * MLA_PAD]
        ms = jnp.sum(x * x, axis=-1, keepdims=True) * (1.0 / MLA_QK)
        y = (x * lax.rsqrt(ms + EPS)) * gq
        qm_ref[:, h * MLA_PAD:h * MLA_PAD + 128] = (y[:, :128] * mla_scale).astype(qm_ref.dtype)
        qm_ref[:, h * MLA_PAD + 128:(h + 1) * MLA_PAD] = (_rope128(y[:, 128:], c, s) * mla_scale).astype(qm_ref.dtype)
    x = z_ref[:, E_LAT:E_LAT + MLA_LAT]
    ms = jnp.mean(x * x, axis=-1, keepdims=True)
    lat = (x * lax.rsqrt(ms + EPS)) * gl_ref[...]
    lat_ref[...] = lat
    zk = z_ref[:, E_KK:E_KK + 128]
    lane = lax.broadcasted_iota(I32, zk.shape, 1)
    kr_in = jnp.where(lane < MLA_ROPE, zk, 0.0)
    ms = jnp.sum(kr_in * kr_in, axis=-1, keepdims=True) * (1.0 / MLA_ROPE)
    kr = _rope128((kr_in * lax.rsqrt(ms + EPS)) * gkr_ref[...], c, s)
    kr_ref[...] = kr[:, :MLA_ROPE]
    ki_ref[...] = zk[:, MLA_ROPE:]
    lk_ref[:, :MLA_LAT] = lat.astype(BF16)
    lk_ref[:, MLA_LAT:] = kr.astype(BF16)
    for h in range(DSA_H):
        x = z_ref[:, E_QD + h * DSA_D:E_QD + (h + 1) * DSA_D]
        ms = jnp.mean(x * x, axis=-1, keepdims=True)
        qd_ref[:, h * DSA_D:(h + 1) * DSA_D] = ((x * lax.rsqrt(ms + EPS)) * gqd_ref[...] * dsa_scale).astype(qd_ref.dtype)
    for h in range(DSA_G):
        x = z_ref[:, E_KD + h * DSA_D:E_KD + (h + 1) * DSA_D]
        ms = jnp.mean(x * x, axis=-1, keepdims=True)
        kd_ref[:, h * DSA_D:(h + 1) * DSA_D] = (x * lax.rsqrt(ms + EPS)) * gkd_ref[...]
    vd_ref[...] = z_ref[:, E_VD:E_VD + DSA_G * DSA_D]
    qi_ref[...] = z_ref[:, E_QI:E_QI + IDX_H * IDX_D].astype(qi_ref.dtype)
    wi_ref[...] = z_ref[:, E_WI:E_WI + 128]


def even_post(z, cos, sin, gq, gl, gkr, gqd, gkd, tm, q_dtype, n_pos_blocks):
    M = z.shape[0]
    row = lambda i: (i, 0)
    pos = lambda i: (i % n_pos_blocks, 0)
    fix = lambda i: (0, 0)
    widths = [(MLA_H * MLA_PAD, q_dtype), (MLA_LAT, F32), (MLA_ROPE, F32), (MLA_LAT + 128, BF16),
              (DSA_H * DSA_D, q_dtype), (DSA_G * DSA_D, F32), (DSA_G * DSA_D, F32),
              (IDX_H * IDX_D, q_dtype), (IDX_D, F32), (128, F32)]
    return pl.pallas_call(
        _even_post_kernel, grid=(M // tm,),
        in_specs=[pl.BlockSpec((tm, E_N), row), pl.BlockSpec((tm, 128), pos), pl.BlockSpec((tm, 128), pos),
                  pl.BlockSpec((1, MLA_PAD), fix), pl.BlockSpec((1, MLA_LAT), fix), pl.BlockSpec((1, 128), fix),
                  pl.BlockSpec((1, DSA_D), fix), pl.BlockSpec((1, DSA_D), fix)],
        out_specs=[pl.BlockSpec((tm, w), row) for w, _ in widths],
        out_shape=[jax.ShapeDtypeStruct((M, w), dt) for w, dt in widths],
        compiler_params=_cp(("parallel",)), name="even_post")(z, cos, sin, gq, gl, gkr, gqd, gkd)


def _log_sigmoid(x):
    return -(jnp.maximum(-x, 0.0) + jnp.log1p(jnp.exp(-jnp.abs(x))))


def _odd_post_kernel(z_ref, gq_ref, gk_ref, bf_ref, u_ref, q_ref, k_ref, v_ref, lf_ref, cum_ref, carry_ref, *, cumulative):
    scale = FOX_D ** -0.5 * LOG2E
    u_ref[...] = z_ref[:, O_U:O_U + S5_CH]
    for h in range(FOX_H):
        x = z_ref[:, O_Q + h * FOX_D:O_Q + (h + 1) * FOX_D]
        ms = jnp.mean(x * x, axis=-1, keepdims=True)
        q_ref[:, h * FOX_D:(h + 1) * FOX_D] = ((x * lax.rsqrt(ms + EPS)) * gq_ref[...] * scale).astype(q_ref.dtype)
    for h in range(FOX_G):
        x = z_ref[:, O_K + h * FOX_D:O_K + (h + 1) * FOX_D]
        ms = jnp.mean(x * x, axis=-1, keepdims=True)
        k_ref[:, h * FOX_D:(h + 1) * FOX_D] = (x * lax.rsqrt(ms + EPS)) * gk_ref[...]
    v_ref[...] = z_ref[:, O_V:O_V + FOX_G * FOX_D]
    zf = z_ref[:, O_F:O_F + 128]
    lane = lax.broadcasted_iota(I32, zf.shape, 1)
    lf = jnp.where(lane < FOX_H, _log_sigmoid(zf + bf_ref[...]), 0.0)
    lf_ref[...] = lf
    if cumulative:
        tm = zf.shape[0]

        @pl.when(pl.program_id(1) == 0)
        def _():
            carry_ref[...] = jnp.zeros_like(carry_ref)

        r = lax.broadcasted_iota(I32, (tm, tm), 0)
        cidx = lax.broadcasted_iota(I32, (tm, tm), 1)
        tri = jnp.where(cidx <= r, 1.0, 0.0).astype(F32)
        cum = jnp.dot(tri, lf, preferred_element_type=F32, precision=lax.Precision.HIGHEST) + carry_ref[...]
        cum_ref[...] = cum * LOG2E
        carry_ref[...] = cum[tm - 1:tm, :]
    else:
        cum_ref[...] = lf


def odd_post(z, gq, gk, bf, tm, q_dtype, n_batch, cumulative):
    M = z.shape[0]
    nb = M // tm // n_batch
    row = lambda b, i: (b * nb + i, 0)
    fix = lambda b, i: (0, 0)
    widths = [(S5_CH, F32), (FOX_H * FOX_D, q_dtype), (FOX_G * FOX_D, F32), (FOX_G * FOX_D, F32), (128, F32), (128, F32)]
    return pl.pallas_call(
        functools.partial(_odd_post_kernel, cumulative=cumulative), grid=(n_batch, nb),
        in_specs=[pl.BlockSpec((tm, O_N), row), pl.BlockSpec((1, FOX_D), fix), pl.BlockSpec((1, FOX_D), fix),
                  pl.BlockSpec((1, 128), fix)],
        out_specs=[pl.BlockSpec((tm, w), row) for w, _ in widths],
        out_shape=[jax.ShapeDtypeStruct((M, w), dt) for w, dt in widths],
        scratch_shapes=[pltpu.VMEM((1, 128), F32)],
        compiler_params=_cp(("arbitrary", "arbitrary")), name="odd_post")(z, gq, gk, bf)


def _flash_kernel(qi_ref, ki_ref, *refs, R, dq, dv, T, TS, shared_kv, has_bias, has_mask):
    q_ref, k_ref, v_ref = refs[:3]
    pos = 3
    cq_ref = ck_ref = mask_ref = None
    if has_bias:
        cq_ref, ck_ref = refs[pos:pos + 2]
        pos += 2
    if has_mask:
        mask_ref = refs[pos]
        pos += 1
    o_ref, m_ref, l_ref, acc_ref = refs[pos:pos + 4]
    t = pl.program_id(2)
    qi = qi_ref[t]
    ki = ki_ref[t]

    @pl.when(ki == 0)
    def _():
        m_ref[...] = jnp.full_like(m_ref, NEG)
        l_ref[...] = jnp.zeros_like(l_ref)
        acc_ref[...] = jnp.zeros_like(acc_ref)

    def step(diag):
        for r in range(R):
            kr = 0 if shared_kv else r
            if r == 0 or not shared_kv:
                k = k_ref[0, :, kr * dq:(kr + 1) * dq].astype(BF16)
                v = v_ref[0, :, kr * dv:(kr + 1) * dv].astype(BF16)
            for qs in range(T // TS):
                rows = slice(qs * TS, (qs + 1) * TS)
                ncol = (qs + 1) * TS if diag else T
                s = _nt(q_ref[0, rows, r * dq:(r + 1) * dq], k[:ncol])
                if has_bias:
                    s = s + (cq_ref[0, 0, rows, r:r + 1] - ck_ref[0, 0, r:r + 1, :ncol])
                if has_mask:
                    mask = mask_ref[0, :, 0].reshape(T, T)
                    s = s + mask[rows, :ncol].astype(F32)
                elif diag:
                    row = qs * TS + lax.broadcasted_iota(I32, (TS, ncol), 0)
                    col = lax.broadcasted_iota(I32, (TS, ncol), 1)
                    s = jnp.where(col <= row, s, NEG)
                m_prev = m_ref[r, rows]
                m_new = jnp.maximum(m_prev, jnp.max(s, axis=1, keepdims=True))
                alpha = jnp.exp2(m_prev - m_new)
                ps = [jnp.exp2(s[:, c * LANES:(c + 1) * LANES] - m_new) for c in range(ncol // LANES)]
                l_ref[r, rows] = alpha * l_ref[r, rows] + functools.reduce(lambda a, b: a + b, ps)
                p = jnp.concatenate(ps, axis=1).astype(BF16)
                acc_ref[r, rows] = alpha * acc_ref[r, rows] + jnp.dot(p, v[:ncol], preferred_element_type=F32)
                m_ref[r, rows] = m_new

    @pl.when(ki < qi)
    def _():
        step(False)

    @pl.when(ki == qi)
    def _():
        step(True)
        for r in range(R):
            l_tot = jnp.sum(l_ref[r], axis=1, keepdims=True)
            o_ref[0, :, r * dv:(r + 1) * dv] = (acc_ref[r] / l_tot).astype(o_ref.dtype)


def flash_attention(q, k, v, *, G, R, dq, dv, T, TS, shared_kv=True, cq=None, ck=None, mask=None, k_block0=0, v_block0=0):
    assert dv == LANES
    B, S, _ = q.shape
    n = S // T
    pairs = [(i, j) for i in range(n) for j in range(i + 1)]
    qi_list = jnp.asarray([p[0] for p in pairs], I32)
    ki_list = jnp.asarray([p[1] for p in pairs], I32)
    kw = 1 if shared_kv else R
    qmap = lambda b, g, t, qi, ki: (b, qi[t], g)
    kmap = lambda b, g, t, qi, ki: (b, ki[t], k_block0 + g)
    vmap = lambda b, g, t, qi, ki: (b, ki[t], v_block0 + g)
    in_specs = [pl.BlockSpec((1, T, R * dq), qmap), pl.BlockSpec((1, T, kw * dq), kmap), pl.BlockSpec((1, T, kw * dv), vmap)]
    args = [q, k, v]
    if cq is not None:
        in_specs += [pl.BlockSpec((1, 1, T, 128), lambda b, g, t, qi, ki: (b, g, qi[t], 0)),
                     pl.BlockSpec((1, 1, 8, T), lambda b, g, t, qi, ki: (b, g, 0, ki[t]))]
        args += [cq, ck]
    if mask is not None:
        tq = mask.shape[3]
        assert mask.shape[4] == T and T % tq == 0
        in_specs += [pl.BlockSpec((1, T // tq, 1, tq, T), lambda b, g, t, qi, ki: (b, qi[t], ki[t], 0, 0))]
        args += [mask]
    kern = functools.partial(_flash_kernel, R=R, dq=dq, dv=dv, T=T, TS=TS, shared_kv=shared_kv,
                             has_bias=cq is not None, has_mask=mask is not None)
    grid_spec = pltpu.PrefetchScalarGridSpec(
        num_scalar_prefetch=2, grid=(B, G, len(pairs)), in_specs=in_specs,
        out_specs=pl.BlockSpec((1, T, R * dv), qmap),
        scratch_shapes=[pltpu.VMEM((R, T, LANES), F32), pltpu.VMEM((R, T, LANES), F32), pltpu.VMEM((R, T, dv), F32)])
    return pl.pallas_call(
        kern, grid_spec=grid_spec, out_shape=jax.ShapeDtypeStruct((B, S, G * R * dv), BF16),
        compiler_params=_cp(("parallel", "parallel", "arbitrary")), name="flash_attention")(qi_list, ki_list, *args)


def _sortable_key(score):
    bits = lax.bitcast_convert_type(score, I32)
    key = bits ^ ((bits >> 31) & jnp.int32(0x7FFFFFFF))
    return jnp.where(key == -1, 0, key)


def _count(mask):
    return jnp.sum(jnp.where(mask, 1.0, 0.0), axis=1, keepdims=True)


def _topk_bias(key_ref, valid, k, j_ref, n_idx_bits):
    rows, cols = key_ref.shape
    kf = float(k)

    def kth(i, t):
        cand = t + jnp.left_shift(jnp.int32(1), 31 - i)
        cnt = _count(key_ref[...] >= cand)
        return jnp.where(cnt >= kf, cand, t)

    thr = lax.fori_loop(0, 32, kth, jnp.full((rows, 1), INT_MIN, I32))
    key = key_ref[...]
    gt = key > thr
    eq = (key == thr) & valid
    need = kf - _count(gt)
    n_eq = _count(eq)
    col = lax.broadcasted_iota(I32, (rows, cols), 1)
    j_ref[...] = jnp.full((rows, 1), cols, I32)
    overfull = jnp.max(jnp.where(n_eq > need, 1.0, 0.0))

    @pl.when(overfull > 0.0)
    def _():
        def cut(i, j0):
            cand = j0 + jnp.left_shift(jnp.int32(1), n_idx_bits - 1 - i)
            cnt = _count((key_ref[...] == thr) & valid & (col < cand))
            return jnp.where(cnt < need, cand, j0)

        j_ref[...] = lax.fori_loop(0, n_idx_bits, cut, jnp.zeros((rows, 1), I32))

    sel = valid & (gt | (eq & (col <= j_ref[...])))
    return jnp.where(sel, 0.0, NEG)


def _dsa_mask_kernel(qi_ref, wi_ref, ki_ref, bias_ref, key_ref, j_ref, *, tq, cw, S):
    qb = pl.program_id(1)
    n_all = S // cw
    nc = (qb * tq + tq - 1) // cw + 1
    q = qi_ref[0]
    w = wi_ref[0]
    kf = float(DSA_TOPK)
    row = qb * tq + lax.broadcasted_iota(I32, (tq, LANES), 0)
    lane = lax.broadcasted_iota(I32, (tq, LANES), 1)
    nsub = cw // LANES

    def score_chunk(c, carry):
        kb = ki_ref[0, pl.ds(pl.multiple_of(c * cw, cw), cw), :].astype(BF16)
        sc = jnp.zeros((tq, cw), F32)
        for h in range(IDX_H):
            sc = sc + w[:, h:h + 1] * jnp.maximum(_nt(q[:, h * IDX_D:(h + 1) * IDX_D], kb), 0.0)
        key = _sortable_key(sc)
        for j in range(nsub):
            col = c * cw + j * LANES + lane
            key_ref[c, :, j * LANES:(j + 1) * LANES] = jnp.where(col <= row, key[:, j * LANES:(j + 1) * LANES], INT_MIN)
        return carry

    lax.fori_loop(0, nc, score_chunk, 0)

    def count(pred):
        def body(c, acc):
            for j in range(nsub):
                hit = pred(key_ref[c, :, j * LANES:(j + 1) * LANES], c * cw + j * LANES + lane)
                acc = acc + jnp.where(hit, 1.0, 0.0)
            return acc
        return jnp.sum(lax.fori_loop(0, nc, body, jnp.zeros((tq, LANES), F32)), axis=1, keepdims=True)

    def kth(i, t):
        cand = t + jnp.left_shift(jnp.int32(1), 31 - i)
        return jnp.where(count(lambda k, col: k >= cand) >= kf, cand, t)

    thr = lax.fori_loop(0, 32, kth, jnp.full((tq, LANES), INT_MIN, I32))
    need = kf - count(lambda k, col: k > thr)
    n_eq = count(lambda k, col: (k == thr) & (col <= row))
    j_ref[...] = jnp.full((tq, LANES), S, I32)
    overfull = jnp.max(jnp.where(n_eq > need, 1.0, 0.0))

    @pl.when(overfull > 0.0)
    def _():
        def cut(i, j0):
            cand = j0 + jnp.left_shift(jnp.int32(1), int(math.log2(S)) - i)
            cnt = count(lambda k, col: (k == thr) & (col <= row) & (col < cand))
            return jnp.where(cnt < need, cand, j0)

        j_ref[...] = lax.fori_loop(0, int(math.log2(S)) + 1, cut, jnp.zeros((tq, LANES), I32))

    jcut = j_ref[...]

    def write_chunk(c, carry):
        for j in range(nsub):
            k = key_ref[c, :, j * LANES:(j + 1) * LANES]
            col = c * cw + j * LANES + lane
            sel = (col <= row) & ((k > thr) | ((k == thr) & (col <= jcut)))
            bias_ref[0, 0, c, :, j * LANES:(j + 1) * LANES] = jnp.where(sel, 0.0, NEG).astype(BF16)
        return carry

    lax.fori_loop(0, nc, write_chunk, 0)

    def fill_chunk(c, carry):
        bias_ref[0, 0, c] = jnp.full((tq, cw), NEG, BF16)
        return carry

    lax.fori_loop(nc, n_all, fill_chunk, 0)


def dsa_prompt_mask(q_idx, w_idx, k_idx, tq, cw):
    B, S, _ = q_idx.shape
    return pl.pallas_call(
        functools.partial(_dsa_mask_kernel, tq=tq, cw=cw, S=S), grid=(B, S // tq),
        in_specs=[pl.BlockSpec((1, tq, IDX_H * IDX_D), lambda b, i: (b, i, 0)),
                  pl.BlockSpec((1, tq, 128), lambda b, i: (b, i, 0)),
                  pl.BlockSpec((1, S, IDX_D), lambda b, i: (b, 0, 0))],
        out_specs=pl.BlockSpec((1, 1, S // cw, tq, cw), lambda b, i: (b, i, 0, 0, 0)),
        out_shape=jax.ShapeDtypeStruct((B, S // tq, S // cw, tq, cw), BF16),
        scratch_shapes=[pltpu.VMEM((S // cw, tq, cw), I32), pltpu.VMEM((tq, LANES), I32)],
        compiler_params=_cp(("parallel", "parallel")), name="dsa_prompt_mask")(q_idx, w_idx, k_idx)


def _s5_params(lam_re, lam_im, log_dt, b_re, b_im, c_re, c_im):
    dt = jnp.exp(log_dt)[:, None]
    def apow(n):
        mag = jnp.exp(lam_re * dt * n)
        return mag * jnp.cos(lam_im * dt * n), mag * jnp.sin(lam_im * dt * n)
    a_re, a_im = apow(1.0)
    den = lam_re * lam_re + lam_im * lam_im
    x, y = a_re - 1.0, a_im
    co_re = (x * lam_re + y * lam_im) / den
    co_im = (y * lam_re - x * lam_im) / den
    bb_re = co_re[..., None] * b_re - co_im[..., None] * b_im
    bb_im = co_re[..., None] * b_im + co_im[..., None] * b_re
    return a_re, a_im, bb_re, bb_im, apow


def _s5_chunk_operators(lam_re, lam_im, log_dt, b_re, b_im, c_re, c_im):
    a_re, a_im, bb_re, bb_im, apow = _s5_params(lam_re, lam_im, log_dt, b_re, b_im, c_re, c_im)
    L = S5_L
    taus = jnp.arange(L + 1, dtype=F32)
    pw = [apow(float(t)) for t in range(L + 1)]
    pw_re = jnp.stack([p[0] for p in pw])
    pw_im = jnp.stack([p[1] for p in pw])
    ab_re = pw_re[..., None] * bb_re[None] - pw_im[..., None] * bb_im[None]
    ab_im = pw_re[..., None] * bb_im[None] + pw_im[..., None] * bb_re[None]
    hp = lax.Precision.HIGHEST
    kern = (jnp.einsum('gcp,tgpd->tgcd', c_re, ab_re[:L], precision=hp)
            - jnp.einsum('gcp,tgpd->tgcd', c_im, ab_im[:L], precision=hp))
    t_out = np.arange(L)[None, :]
    s_in = np.arange(L)[:, None]
    lag = np.clip(t_out - s_in, 0, L - 1)
    m = kern[lag]
    m = jnp.where((t_out >= s_in)[:, :, None, None, None], m, 0.0)
    m = m.transpose(2, 0, 4, 1, 3).reshape(S5_G, L * S5_GRP, L * S5_GRP)
    w_re = ab_re[L - 1 - np.arange(L)]
    w_im = ab_im[L - 1 - np.arange(L)]
    w = jnp.concatenate([w_re, w_im], axis=2)
    w = w.transpose(1, 0, 3, 2).reshape(S5_G, L * S5_GRP, 2 * S5_P)
    ca_re = c_re[None] * pw_re[1:, :, None, :] - c_im[None] * pw_im[1:, :, None, :]
    ca_im = c_re[None] * pw_im[1:, :, None, :] + c_im[None] * pw_re[1:, :, None, :]
    v = jnp.concatenate([ca_re, -ca_im], axis=3)
    v = v.transpose(1, 3, 0, 2).reshape(S5_G, 2 * S5_P, L * S5_GRP)
    return m.astype(BF16), w.astype(BF16), v.astype(BF16), apow


def _s5_chunk_kernel(u_ref, m_ref, w_ref, v_ref, apr_ref, api_ref, y_ref, xr_ref, xi_ref, *, n_chunks, n_batch):
    u = u_ref[0].astype(BF16)
    y = jnp.dot(u, m_ref[0], preferred_element_type=F32)
    d = jnp.dot(u, w_ref[0], preferred_element_type=F32)
    xr = d[:, :S5_P]
    xi = d[:, S5_P:]
    rows = n_chunks * n_batch
    cidx = lax.broadcasted_iota(I32, (rows, S5_P), 0) % n_chunks
    for k in range(int(math.log2(n_chunks))):
        sh = 1 << k
        ar = apr_ref[0, k:k + 1, :]
        ai = api_ref[0, k:k + 1, :]
        keep = cidx >= sh
        sr = jnp.where(keep, pltpu.roll(xr, sh, 0), 0.0)
        si = jnp.where(keep, pltpu.roll(xi, sh, 0), 0.0)
        xr, xi = xr + ar * sr - ai * si, xi + ar * si + ai * sr
    keep = cidx >= 1
    xin = jnp.concatenate([jnp.where(keep, pltpu.roll(xr, 1, 0), 0.0),
                           jnp.where(keep, pltpu.roll(xi, 1, 0), 0.0)], axis=1).astype(BF16)
    y_ref[0] = y + jnp.dot(xin, v_ref[0], preferred_element_type=F32)
    xr_ref[0] = jnp.concatenate([xr[(b + 1) * n_chunks - 1:(b + 1) * n_chunks, :] for b in range(n_batch)], axis=0)
    xi_ref[0] = jnp.concatenate([xi[(b + 1) * n_chunks - 1:(b + 1) * n_chunks, :] for b in range(n_batch)], axis=0)


def s5_prompt(u, lam_re, lam_im, log_dt, b_re, b_im, c_re, c_im):
    B, T, _ = u.shape
    L = S5_L
    nch = T // L
    m, w, v, apow = _s5_chunk_operators(lam_re, lam_im, log_dt, b_re, b_im, c_re, c_im)
    nlev = int(math.log2(nch))
    ap = [apow(float(L * (1 << k))) for k in range(nlev)]
    apr = jnp.stack([p[0] for p in ap], axis=1)
    api = jnp.stack([p[1] for p in ap], axis=1)
    ug = u.reshape(B, nch, L, S5_G, S5_GRP).transpose(3, 0, 1, 2, 4).reshape(S5_G, B * nch, L * S5_GRP)
    rows = B * nch
    g3 = lambda g: (g, 0, 0)
    y, xr, xi = pl.pallas_call(
        functools.partial(_s5_chunk_kernel, n_chunks=nch, n_batch=B), grid=(S5_G,),
        in_specs=[pl.BlockSpec((1, rows, L * S5_GRP), g3), pl.BlockSpec((1, L * S5_GRP, L * S5_GRP), g3),
                  pl.BlockSpec((1, L * S5_GRP, 2 * S5_P), g3), pl.BlockSpec((1, 2 * S5_P, L * S5_GRP), g3),
                  pl.BlockSpec((1, nlev, S5_P), g3), pl.BlockSpec((1, nlev, S5_P), g3)],
        out_specs=[pl.BlockSpec((1, rows, L * S5_GRP), g3), pl.BlockSpec((1, B, S5_P), g3), pl.BlockSpec((1, B, S5_P), g3)],
        out_shape=[jax.ShapeDtypeStruct((S5_G, rows, L * S5_GRP), F32),
                   jax.ShapeDtypeStruct((S5_G, B, S5_P), F32), jax.ShapeDtypeStruct((S5_G, B, S5_P), F32)],
        compiler_params=_cp(("parallel",)), name="s5_chunk_scan")(ug, m, w, v, apr, api)
    y = y.reshape(S5_G, B, nch, L, S5_GRP).transpose(1, 2, 3, 0, 4).reshape(B, T, S5_CH)
    return y, xr.transpose(1, 0, 2), xi.transpose(1, 0, 2)


def _gelu_tanh(x):
    return 0.5 * x * (1.0 + jnp.tanh(math.sqrt(2.0 / math.pi) * (x + 0.044715 * (x * x * x))))


def _s5_glu_kernel(y_ref, u_ref, d_ref, w_ref, b_ref, o_ref):
    h = _gelu_tanh(y_ref[...] + d_ref[...] * u_ref[...])
    gate = jnp.dot(h.astype(BF16), w_ref[...], preferred_element_type=F32) + b_ref[...]
    o_ref[...] = (h * (1.0 / (1.0 + jnp.exp(-gate)))).astype(o_ref.dtype)


def s5_glu(y, u, d, w_glu, b_glu, tm):
    M, C = y.shape
    row = lambda i: (i, 0)
    fix = lambda i: (0, 0)
    return pl.pallas_call(
        _s5_glu_kernel, grid=(M // tm,),
        in_specs=[pl.BlockSpec((tm, C), row), pl.BlockSpec((tm, C), row), pl.BlockSpec((1, C), fix),
                  pl.BlockSpec((C, C), fix), pl.BlockSpec((1, C), fix)],
        out_specs=pl.BlockSpec((tm, C), row),
        out_shape=jax.ShapeDtypeStruct((M, C), BF16),
        compiler_params=_cp(("parallel",)), name="s5_glu")(y, u, d, w_glu, b_glu)


S5_GB = 8


def _s5_step_kernel(u_ref, x0r_ref, x0i_ref, ar_ref, ai_ref, br_ref, bi_ref, cr_ref, ci_ref, y_ref, xr_ref, xi_ref):
    u = u_ref[...].astype(BF16)
    ar, ai = ar_ref[...], ai_ref[...]
    x0r, x0i = x0r_ref[...], x0i_ref[...]
    xr = ar * x0r - ai * x0i + jnp.dot(u, br_ref[0], preferred_element_type=F32)
    xi = ar * x0i + ai * x0r + jnp.dot(u, bi_ref[0], preferred_element_type=F32)
    xr_ref[...] = xr
    xi_ref[...] = xi
    y_ref[...] = (jnp.dot(xr.astype(BF16), cr_ref[0], preferred_element_type=F32)
                  - jnp.dot(xi.astype(BF16), ci_ref[0], preferred_element_type=F32))


def s5_sample(u, x0_re, x0_im, lam_re, lam_im, log_dt, b_re, b_im, c_re, c_im):
    N = u.shape[0]
    a_re, a_im, bb_re, bb_im, _ = _s5_params(lam_re, lam_im, log_dt, b_re, b_im, c_re, c_im)
    nb = S5_G // S5_GB
    eye = jnp.eye(S5_GB, dtype=F32)
    bd = lambda t: jnp.einsum('jqpc,qr->jqcrp', t.reshape(nb, S5_GB, S5_P, S5_GRP), eye).reshape(
        nb, S5_GB * S5_GRP, S5_GB * S5_P).astype(BF16)
    cd = lambda t: jnp.einsum('jqcp,qr->jrpqc', t.reshape(nb, S5_GB, S5_GRP, S5_P), eye).reshape(
        nb, S5_GB * S5_P, S5_GB * S5_GRP).astype(BF16)
    wu, ws = S5_GB * S5_GRP, S5_GB * S5_P
    col = lambda j: (0, j)
    blk = lambda j: (j, 0, 0)
    y, xr, xi = pl.pallas_call(
        _s5_step_kernel, grid=(nb,),
        in_specs=[pl.BlockSpec((N, wu), col), pl.BlockSpec((N, ws), col), pl.BlockSpec((N, ws), col),
                  pl.BlockSpec((1, ws), col), pl.BlockSpec((1, ws), col),
                  pl.BlockSpec((1, wu, ws), blk), pl.BlockSpec((1, wu, ws), blk),
                  pl.BlockSpec((1, ws, wu), blk), pl.BlockSpec((1, ws, wu), blk)],
        out_specs=[pl.BlockSpec((N, wu), col), pl.BlockSpec((N, ws), col), pl.BlockSpec((N, ws), col)],
        out_shape=[jax.ShapeDtypeStruct((N, S5_CH), F32), jax.ShapeDtypeStruct((N, S5_G * S5_P), F32),
                   jax.ShapeDtypeStruct((N, S5_G * S5_P), F32)],
        compiler_params=_cp(("parallel",)), name="s5_step")(
            u, x0_re.reshape(N, -1), x0_im.reshape(N, -1), a_re.reshape(1, -1), a_im.reshape(1, -1),
            bd(bb_re), bd(bb_im), cd(c_re), cd(c_im))
    return y, xr.reshape(N, S5_G, S5_P), xi.reshape(N, S5_G, S5_P)


def _fetch(pt_ref, n, n_chunks, pages_per_chunk, slot, copies, sem, wait):
    b = n // n_chunks
    c = n % n_chunks

    def body(p, carry):
        page = 0 if wait else pt_ref[b, c * pages_per_chunk + p]
        for src, dst in copies:
            cp = pltpu.make_async_copy(src(page), dst(slot, p), sem.at[slot])
            if wait:
                cp.wait()
            else:
                cp.start()
        return carry

    lax.fori_loop(0, pages_per_chunk, body, 0)


def _pipeline_pages(pt_ref, n_chunks, pages_per_chunk, copies, sem):
    n = pl.program_id(0)
    slot = n % 2

    @pl.when(n == 0)
    def _():
        _fetch(pt_ref, n, n_chunks, pages_per_chunk, slot, copies, sem, False)

    @pl.when(n + 1 < pl.num_programs(0))
    def _():
        _fetch(pt_ref, n + 1, n_chunks, pages_per_chunk, 1 - slot, copies, sem, False)

    _fetch(pt_ref, n, n_chunks, pages_per_chunk, slot, copies, sem, True)
    return slot


def _softmax_update(s, v, m_ref, l_ref, acc_ref, idx):
    m_prev = m_ref[idx]
    m_new = jnp.maximum(m_prev, jnp.max(s, axis=-1, keepdims=True))
    alpha = jnp.exp2(m_prev - m_new)
    p = jnp.exp2(s - m_new)
    l_ref[idx] = alpha * l_ref[idx] + jnp.sum(p, axis=-1, keepdims=True)
    acc_ref[idx] = alpha * acc_ref[idx] + jnp.dot(p.astype(BF16), v, preferred_element_type=F32)
    m_ref[idx] = m_new


def _softmax_finish(s_new, v_new, m_ref, l_ref, acc_ref, idx):
    m_prev = m_ref[idx]
    m_new = jnp.maximum(m_prev, s_new)
    alpha = jnp.exp2(m_prev - m_new)
    p = jnp.exp2(s_new - m_new)
    return (alpha * acc_ref[idx] + p * v_new) / (alpha * l_ref[idx] + p)


def _init_softmax(c, m_ref, l_ref, acc_ref):
    @pl.when(c == 0)
    def _():
        m_ref[...] = jnp.full_like(m_ref, NEG)
        l_ref[...] = jnp.zeros_like(l_ref)
        acc_ref[...] = jnp.zeros_like(acc_ref)


def _mla_decode_kernel(pt_ref, ql_ref, qr_ref, ln_ref, kn_ref, clat, ckr, o_ref,
                       latbuf, krbuf, sem, m_ref, l_ref, acc_ref, *, n_chunks, ppc, li):
    copies = [(lambda pg: clat.at[li, pg], lambda s, p: latbuf.at[s, p]),
              (lambda pg: ckr.at[li, pg], lambda s, p: krbuf.at[s, :, pl.ds(pl.multiple_of(p * PAGE, PAGE), PAGE)])]
    slot = _pipeline_pages(pt_ref, n_chunks, ppc, copies, sem)
    c = pl.program_id(0) % n_chunks
    _init_softmax(c, m_ref, l_ref, acc_ref)
    L = ppc * PAGE
    klat = latbuf[slot].reshape(L, MLA_LAT).astype(BF16)
    kkr_t = krbuf[slot].astype(BF16)
    ql = ql_ref[0]
    qr = qr_ref[0]
    s = _nt(ql.astype(BF16), klat) + jnp.dot(qr.astype(BF16), kkr_t, preferred_element_type=F32)
    _softmax_update(s, klat, m_ref, l_ref, acc_ref, 0)

    @pl.when(c == n_chunks - 1)
    def _():
        ln = ln_ref[0]
        s_new = jnp.sum(ql * ln, axis=-1, keepdims=True) + jnp.sum(qr * kn_ref[0], axis=-1, keepdims=True)
        o_ref[0] = _softmax_finish(s_new, ln, m_ref, l_ref, acc_ref, 0)


def mla_decode(page_table, q_lat, q_rope, lat_new, kr_new, cache_lat, cache_kr_t, li, ppc):
    N, n_pages = page_table.shape
    n_chunks = n_pages // ppc
    row = lambda n, pt: (n // n_chunks, 0, 0)
    grid_spec = pltpu.PrefetchScalarGridSpec(
        num_scalar_prefetch=1, grid=(N * n_chunks,),
        in_specs=[pl.BlockSpec((1, MLA_H, MLA_LAT), row), pl.BlockSpec((1, MLA_H, MLA_ROPE), row),
                  pl.BlockSpec((1, 1, MLA_LAT), row), pl.BlockSpec((1, 1, MLA_ROPE), row),
                  pl.BlockSpec(memory_space=pl.ANY), pl.BlockSpec(memory_space=pl.ANY)],
        out_specs=pl.BlockSpec((1, MLA_H, MLA_LAT), row),
        scratch_shapes=[pltpu.VMEM((2, ppc, PAGE, MLA_LAT), F32), pltpu.VMEM((2, MLA_ROPE, ppc * PAGE), F32),
                        pltpu.SemaphoreType.DMA((2,)),
                        pltpu.VMEM((1, MLA_H, 1), F32), pltpu.VMEM((1, MLA_H, 1), F32), pltpu.VMEM((1, MLA_H, MLA_LAT), F32)])
    return pl.pallas_call(
        functools.partial(_mla_decode_kernel, n_chunks=n_chunks, ppc=ppc, li=li), grid_spec=grid_spec,
        out_shape=jax.ShapeDtypeStruct((N, MLA_H, MLA_LAT), F32),
        compiler_params=_cp(("arbitrary",)), name="mla_decode")(page_table, q_lat, q_rope, lat_new, kr_new, cache_lat, cache_kr_t)


def _gqa_decode_kernel(*refs, n_chunks, ppc, li, G, R, D, bias_mode):
    pt_ref, q_ref, kn_ref, vn_ref = refs[:4]
    if bias_mode == 'mask':
        bp_ref, bn_ref = refs[4:6]
    else:
        sfx_ref, cn_ref = refs[4:6]
    ck, cv, o_ref, kbuf, vbuf, sem, m_ref, l_ref, acc_ref = refs[6:]
    copies = []
    for g in range(G):
        copies.append((lambda pg, g=g: ck.at[li, pg, :, g, :], lambda s, p, g=g: kbuf.at[s, g, p]))
        copies.append((lambda pg, g=g: cv.at[li, pg, :, g, :], lambda s, p, g=g: vbuf.at[s, g, p]))
    slot = _pipeline_pages(pt_ref, n_chunks, ppc, copies, sem)
    c = pl.program_id(0) % n_chunks
    _init_softmax(c, m_ref, l_ref, acc_ref)
    L = ppc * PAGE
    q = q_ref[0]
    for g in range(G):
        k = kbuf[slot, g].reshape(L, D).astype(BF16)
        v = vbuf[slot, g].reshape(L, D).astype(BF16)
        s = _nt(q[g * R:(g + 1) * R].astype(BF16), k)
        if bias_mode == 'mask':
            s = s + bp_ref[0]
        else:
            sfx = jnp.concatenate([sfx_ref[0, p, g * R:(g + 1) * R, :] for p in range(ppc)], axis=1)
            s = s + (sfx + cn_ref[0, g * R:(g + 1) * R, :]) * LOG2E
        _softmax_update(s, v, m_ref, l_ref, acc_ref, g)

    @pl.when(c == n_chunks - 1)
    def _():
        for g in range(G):
            qg = q[g * R:(g + 1) * R]
            s_new = jnp.sum(qg * kn_ref[0, g:g + 1, :], axis=-1, keepdims=True)
            if bias_mode == 'mask':
                s_new = s_new + bn_ref[0, :, 0:1]
            o_ref[0, g * R:(g + 1) * R, :] = _softmax_finish(s_new, vn_ref[0, g:g + 1, :], m_ref, l_ref, acc_ref, g)


def gqa_decode(page_table, q, k_new, v_new, bias_a, bias_b, cache_k, cache_v, li, ppc, bias_mode):
    N, n_pages = page_table.shape
    n_chunks = n_pages // ppc
    H, D = q.shape[1:]
    G = k_new.shape[1]
    R = H // G
    row = lambda n, pt: (n // n_chunks, 0, 0)
    if bias_mode == 'mask':
        bias_specs = [pl.BlockSpec((1, 1, ppc * PAGE), lambda n, pt: (n // n_chunks, 0, n % n_chunks)),
                      pl.BlockSpec((1, 1, 128), row)]
    else:
        bias_specs = [pl.BlockSpec((1, ppc, H, PAGE), lambda n, pt: (n // n_chunks, n % n_chunks, 0, 0)),
                      pl.BlockSpec((1, H, 1), row)]
    grid_spec = pltpu.PrefetchScalarGridSpec(
        num_scalar_prefetch=1, grid=(N * n_chunks,),
        in_specs=[pl.BlockSpec((1, H, D), row), pl.BlockSpec((1, G, D), row), pl.BlockSpec((1, G, D), row)]
        + bias_specs + [pl.BlockSpec(memory_space=pl.ANY), pl.BlockSpec(memory_space=pl.ANY)],
        out_specs=pl.BlockSpec((1, H, D), row),
        scratch_shapes=[pltpu.VMEM((2, G, ppc, PAGE, D), F32), pltpu.VMEM((2, G, ppc, PAGE, D), F32),
                        pltpu.SemaphoreType.DMA((2,)),
                        pltpu.VMEM((G, R, 1), F32), pltpu.VMEM((G, R, 1), F32), pltpu.VMEM((G, R, D), F32)])
    kern = functools.partial(_gqa_decode_kernel, n_chunks=n_chunks, ppc=ppc, li=li, G=G, R=R, D=D, bias_mode=bias_mode)
    return pl.pallas_call(
        kern, grid_spec=grid_spec, out_shape=jax.ShapeDtypeStruct((N, H, D), F32),
        compiler_params=_cp(("arbitrary",)), name="gqa_decode_" + bias_mode)(
            page_table, q, k_new, v_new, bias_a, bias_b, cache_k, cache_v)


def _dsa_score_kernel(pt_ref, qi_ref, wi_ref, cidx, o_ref, kbuf, sem, *, n_chunks, ppc, li):
    copies = [(lambda pg: cidx.at[li, pg], lambda s, p: kbuf.at[s, :, pl.ds(pl.multiple_of(p * PAGE, PAGE), PAGE)])]
    slot = _pipeline_pages(pt_ref, n_chunks, ppc, copies, sem)
    k_t = kbuf[slot].astype(BF16)
    d = jnp.maximum(jnp.dot(qi_ref[0].astype(BF16), k_t, preferred_element_type=F32), 0.0)
    o_ref[0] = jnp.sum(wi_ref[0] * d, axis=0, keepdims=True)


def dsa_sample_scores(page_table, q_idx, w_idx, cache_idx_t, li, ppc):
    N, n_pages = page_table.shape
    n_chunks = n_pages // ppc
    row = lambda n, pt: (n // n_chunks, 0, 0)
    grid_spec = pltpu.PrefetchScalarGridSpec(
        num_scalar_prefetch=1, grid=(N * n_chunks,),
        in_specs=[pl.BlockSpec((1, IDX_H, IDX_D), row), pl.BlockSpec((1, IDX_H, 1), row), pl.BlockSpec(memory_space=pl.ANY)],
        out_specs=pl.BlockSpec((1, 1, ppc * PAGE), lambda n, pt: (n // n_chunks, 0, n % n_chunks)),
        scratch_shapes=[pltpu.VMEM((2, IDX_D, ppc * PAGE), F32), pltpu.SemaphoreType.DMA((2,))])
    return pl.pallas_call(
        functools.partial(_dsa_score_kernel, n_chunks=n_chunks, ppc=ppc, li=li), grid_spec=grid_spec,
        out_shape=jax.ShapeDtypeStruct((N, 1, n_pages * PAGE), F32),
        compiler_params=_cp(("arbitrary",)), name="dsa_sample_scores")(page_table, q_idx, w_idx, cache_idx_t)


def _dsa_select_kernel(sp_ref, qi_ref, wi_ref, kin_ref, bp_ref, bn_ref, key_ref, j_ref, *, P):
    rows = sp_ref.shape[0]
    q = qi_ref[...]
    kn = kin_ref[...]
    w = wi_ref[...]
    s_new = jnp.zeros((rows, 1), F32)
    for h in range(IDX_H):
        d = jnp.sum(q[:, h * IDX_D:(h + 1) * IDX_D].astype(BF16).astype(F32) * kn.astype(BF16).astype(F32),
                    axis=-1, keepdims=True)
        s_new = s_new + w[:, h:h + 1] * jnp.maximum(d, 0.0)
    lane = lax.broadcasted_iota(I32, (rows, 128), 1)
    key_ref[:, :P] = _sortable_key(sp_ref[...])
    key_ref[:, P:] = jnp.where(lane == 0, _sortable_key(jnp.broadcast_to(s_new, (rows, 128))), INT_MIN)
    col = lax.broadcasted_iota(I32, (rows, P + 128), 1)
    bias = _topk_bias(key_ref, col <= P, DSA_TOPK, j_ref, int(math.log2(P)) + 1)
    bp_ref[...] = bias[:, :P]
    bn_ref[...] = bias[:, P:]


def dsa_sample_select(scores_past, q_idx, w_idx, k_idx_new, tr):
    N, P = scores_past.shape
    row = lambda i: (i, 0)
    return pl.pallas_call(
        functools.partial(_dsa_select_kernel, P=P), grid=(N // tr,),
        in_specs=[pl.BlockSpec((tr, P), row), pl.BlockSpec((tr, IDX_H * IDX_D), row), pl.BlockSpec((tr, 128), row),
                  pl.BlockSpec((tr, IDX_D), row)],
        out_specs=[pl.BlockSpec((tr, P), row), pl.BlockSpec((tr, 128), row)],
        out_shape=[jax.ShapeDtypeStruct((N, P), F32), jax.ShapeDtypeStruct((N, 128), F32)],
        scratch_shapes=[pltpu.VMEM((tr, P + 128), I32), pltpu.VMEM((tr, 1), I32)],
        compiler_params=_cp(("parallel",)), name="dsa_sample_select")(scores_past, q_idx, w_idx, k_idx_new)


def _fox_suffix_kernel(pt_ref, clf, o_ref, buf, sem, ins_ref, tot_ref, *, n_pages, li):
    copies = [(lambda pg: clf.at[li, pg], lambda s, p: buf.at[s, p])]
    slot = _pipeline_pages(pt_ref, 1, n_pages, copies, sem)
    jj = lax.broadcasted_iota(I32, (PAGE, PAGE), 0)
    pp = lax.broadcasted_iota(I32, (PAGE, PAGE), 1)
    later = jnp.where(jj > pp, 1.0, 0.0).astype(F32)
    ones = jnp.ones((PAGE, PAGE), F32)
    hp = lax.Precision.HIGHEST
    lf = buf[slot].reshape(n_pages * FOX_H, PAGE)
    ins_ref[...] = jnp.dot(lf, later, preferred_element_type=F32, precision=hp).reshape(n_pages, FOX_H, PAGE)
    tot_ref[...] = jnp.dot(lf, ones, preferred_element_type=F32, precision=hp).reshape(n_pages, FOX_H, PAGE)

    def body(i, carry):
        p = n_pages - 1 - i
        o_ref[0, p] = ins_ref[p] + carry
        return carry + tot_ref[p]

    lax.fori_loop(0, n_pages, body, jnp.zeros((FOX_H, PAGE), F32))


def fox_suffix(page_table, cache_logf_t, li):
    N, n_pages = page_table.shape
    grid_spec = pltpu.PrefetchScalarGridSpec(
        num_scalar_prefetch=1, grid=(N,),
        in_specs=[pl.BlockSpec(memory_space=pl.ANY)],
        out_specs=pl.BlockSpec((1, n_pages, FOX_H, PAGE), lambda n, pt: (n, 0, 0, 0)),
        scratch_shapes=[pltpu.VMEM((2, n_pages, FOX_H, PAGE), F32), pltpu.SemaphoreType.DMA((2,)),
                        pltpu.VMEM((n_pages, FOX_H, PAGE), F32), pltpu.VMEM((n_pages, FOX_H, PAGE), F32)])
    return pl.pallas_call(
        functools.partial(_fox_suffix_kernel, n_pages=n_pages, li=li), grid_spec=grid_spec,
        out_shape=jax.ShapeDtypeStruct((N, n_pages, FOX_H, PAGE), F32),
        compiler_params=_cp(("arbitrary",)), name="fox_suffix")(page_table, cache_logf_t)


def _per_head_kernel(a_ref, w_ref, o_ref, *, transpose_w):
    a = a_ref[...].astype(BF16)
    w = w_ref[...].astype(BF16)
    o_ref[...] = _nt(a, w) if transpose_w else jnp.dot(a, w, preferred_element_type=F32)


def per_head_matmul(a, w, d_in, d_out, transpose_w):
    N = a.shape[0]
    H = a.shape[1] // d_in
    wb = (d_out, d_in) if transpose_w else (d_in, d_out)
    return pl.pallas_call(
        functools.partial(_per_head_kernel, transpose_w=transpose_w), grid=(H,),
        in_specs=[pl.BlockSpec((N, d_in), lambda h: (0, h)), pl.BlockSpec(wb, lambda h: (0, h))],
        out_specs=pl.BlockSpec((N, d_out), lambda h: (0, h)),
        out_shape=jax.ShapeDtypeStruct((N, H * d_out), F32),
        compiler_params=_cp(("parallel",)), name="per_head_matmul")(a, w)


def _rope_tables(pos):
    half = MLA_ROPE // 2
    inv_freq = ROPE_BASE ** (-jnp.arange(half, dtype=F32) / half)
    ang = pos.astype(F32)[:, None] * inv_freq
    cos, sin = jnp.cos(ang), jnp.sin(ang)
    z = jnp.zeros((pos.shape[0], 128 - MLA_ROPE), F32)
    return jnp.concatenate([cos, cos, z], axis=1), jnp.concatenate([-sin, sin, z], axis=1)


def _pad_cols(a, n):
    return jnp.pad(a, ((0, 0), (0, n - a.shape[1])))


def _even_weights(w_in, g_q_mla, g_krope, w_uk, w_uv):
    D = w_in.shape[0]
    o = 0
    zq = w_in[:, o:o + MLA_H * MLA_QK].reshape(D, MLA_H, MLA_QK); o += MLA_H * MLA_QK
    zq = jnp.pad(zq, ((0, 0), (0, 0), (0, MLA_PAD - MLA_QK))).reshape(D, MLA_H * MLA_PAD)
    lat = w_in[:, o:o + MLA_LAT]; o += MLA_LAT
    kr = w_in[:, o:o + MLA_ROPE]; o += MLA_ROPE
    qd = w_in[:, o:o + DSA_H * DSA_D]; o += DSA_H * DSA_D
    kd = w_in[:, o:o + DSA_G * DSA_D]; o += DSA_G * DSA_D
    vd = w_in[:, o:o + DSA_G * DSA_D]; o += DSA_G * DSA_D
    qi = w_in[:, o:o + IDX_H * IDX_D]; o += IDX_H * IDX_D
    ki = w_in[:, o:o + IDX_D]; o += IDX_D
    wi = w_in[:, o:o + IDX_H]
    w = jnp.concatenate([zq, lat, qd, kd, vd, qi, kr, ki, _pad_cols(wi, 128)], axis=1).astype(BF16)
    gq = _pad_cols(g_q_mla[None, :], MLA_PAD)
    gkr = _pad_cols(g_krope[None, :], 128)
    wk = jnp.zeros((MLA_LAT + 128, MLA_H, MLA_PAD), F32)
    wk = wk.at[:MLA_LAT, :, :MLA_NOPE].set(w_uk)
    wk = wk.at[MLA_LAT:MLA_LAT + MLA_ROPE, :, MLA_NOPE:MLA_QK].set(
        jnp.broadcast_to(jnp.eye(MLA_ROPE, dtype=F32)[:, None, :], (MLA_ROPE, MLA_H, MLA_ROPE)))
    wv = jnp.zeros((MLA_LAT + 128, MLA_H * MLA_V), F32).at[:MLA_LAT].set(w_uv.reshape(MLA_LAT, MLA_H * MLA_V))
    wkv = jnp.concatenate([wk.reshape(MLA_LAT + 128, MLA_H * MLA_PAD), wv], axis=1).astype(BF16)
    return w, gq, gkr, wkv


def _odd_weights(w_in, fox_b_f):
    w = _pad_cols(w_in, O_N).astype(BF16)
    return w, _pad_cols(fox_b_f[None, :], 128)


def kernel(x_prompt, x_sample, cache_mla_latent, cache_mla_krope, cache_dsa_k, cache_dsa_v, cache_dsa_idx_k, cache_fox_k, cache_fox_v, cache_fox_logf, state_s5_re, state_s5_im, page_table, ln_mix_even, w_in_even, g_q_mla, g_latent, g_krope, w_uk, w_uv, g_q_dsa, g_k_dsa, w_out_even, ln_mix_odd, w_in_odd, s5_lam_re, s5_lam_im, s5_log_dt, s5_b_re, s5_b_im, s5_c_re, s5_c_im, s5_d, s5_w_glu, s5_b_glu, g_q_fox, g_k_fox, fox_b_f, w_out_odd, ln_mlp, w_up, w_down):
    B, S, D = x_prompt.shape
    N = x_sample.shape[0]
    n_pages = page_table.shape[1]
    P = n_pages * PAGE
    MP = B * S
    TM = 512
    TA = 512
    TS = 512
    PPC = 64
    yp = x_prompt.reshape(MP, D)
    ys = x_sample.reshape(N, D)
    cos_p, sin_p = _rope_tables(jnp.arange(S))
    cos_s, sin_s = _rope_tables(jnp.full((N,), P, I32))
    outs = {}
    depth = ln_mlp.shape[0]
    for layer in range(depth):
        li = layer // 2
        if layer % 2 == 0:
            w, gq, gkr, wkv = _even_weights(w_in_even[li], g_q_mla[li], g_krope[li], w_uk[li], w_uv[li])
            gains = (gq, g_latent[li][None], gkr, g_q_dsa[li][None], g_k_dsa[li][None])
            w_out = w_out_even[li].astype(BF16)
            w_o1, w_o2 = w_out[:MLA_H * MLA_V], w_out[MLA_H * MLA_V:]
            z = rms_matmul(yp, ln_mix_even[li][None], w, TM, E_N // 2)
            qm, lat, kr, lk, qd, kd, vd, qi, ki, wi = even_post(z, cos_p, sin_p, *gains, 256, BF16, S // 256)
            kv = matmul(lk, wkv, BF16, TM, wkv.shape[1] // 3)
            r3 = lambda a: a.reshape(B, S, a.shape[1])
            RM = MLA_H // 2
            o_mla = flash_attention(r3(qm), r3(kv), r3(kv), G=2, R=RM, dq=MLA_PAD, dv=MLA_V, T=TA, TS=TS,
                                    shared_kv=False, v_block0=MLA_H * MLA_PAD // (RM * MLA_V))
            mask = dsa_prompt_mask(r3(qi), r3(wi), r3(ki), 128, TA)
            o_dsa = flash_attention(r3(qd), r3(kd), r3(vd), G=DSA_G, R=DSA_H // DSA_G, dq=DSA_D, dv=DSA_D, T=TA, TS=TS,
                                    mask=mask)
            yp = out_proj(yp, o_mla.reshape(MP, -1), o_dsa.reshape(MP, -1), w_o1, w_o2, TM, 1024)
            outs.setdefault('lat_p', []).append(lat.reshape(B, S, MLA_LAT))
            outs.setdefault('kr_p', []).append(kr.reshape(B, S, MLA_ROPE))
            outs.setdefault('dk_p', []).append(kd.reshape(B, S, DSA_G, DSA_D))
            outs.setdefault('dv_p', []).append(vd.reshape(B, S, DSA_G, DSA_D))
            outs.setdefault('di_p', []).append(ki.reshape(B, S, IDX_D))
            z = rms_matmul(ys, ln_mix_even[li][None], w, N, E_N // 2)
            qm, lat, kr, lk, qd, kd, vd, qi, ki, wi = even_post(z, cos_s, sin_s, *gains, N, F32, 1)
            qm3 = qm.reshape(N, MLA_H, MLA_PAD)
            q_nope = qm3[:, :, :MLA_NOPE].reshape(N, MLA_H * MLA_NOPE)
            q_rope = qm3[:, :, MLA_NOPE:MLA_QK]
            w_uk2 = w_uk[li].reshape(MLA_LAT, MLA_H * MLA_NOPE)
            q_lat = per_head_matmul(q_nope, w_uk2, MLA_NOPE, MLA_LAT, True).reshape(N, MLA_H, MLA_LAT)
            o_lat = mla_decode(page_table, q_lat, q_rope, lat[:, None, :], kr[:, None, :],
                               cache_mla_latent, jnp.swapaxes(cache_mla_krope, 2, 3), li, PPC)
            o_mla = per_head_matmul(o_lat.reshape(N, MLA_H * MLA_LAT), w_uv[li].reshape(MLA_LAT, MLA_H * MLA_V),
                                    MLA_LAT, MLA_V, False)
            sc = dsa_sample_scores(page_table, qi.reshape(N, IDX_H, IDX_D), wi[:, :IDX_H, None],
                                   jnp.swapaxes(cache_dsa_idx_k, 2, 3), li, PPC)
            bias_p, bias_n = dsa_sample_select(sc.reshape(N, P), qi, wi, ki, 32)
            o_dsa = gqa_decode(page_table, qd.reshape(N, DSA_H, DSA_D), kd.reshape(N, DSA_G, DSA_D),
                               vd.reshape(N, DSA_G, DSA_D), bias_p[:, None, :], bias_n[:, None, :],
                               cache_dsa_k, cache_dsa_v, li, PPC, 'mask')
            ys = out_proj(ys, o_mla, o_dsa.reshape(N, -1), w_o1, w_o2, N, 1024)
            outs.setdefault('lat_s', []).append(lat.reshape(N, 1, MLA_LAT))
            outs.setdefault('kr_s', []).append(kr.reshape(N, 1, MLA_ROPE))
            outs.setdefault('dk_s', []).append(kd.reshape(N, 1, DSA_G, DSA_D))
            outs.setdefault('dv_s', []).append(vd.reshape(N, 1, DSA_G, DSA_D))
            outs.setdefault('di_s', []).append(ki.reshape(N, 1, IDX_D))
        else:
            w, bf = _odd_weights(w_in_odd[li], fox_b_f[li])
            s5p = (s5_lam_re[li], s5_lam_im[li], s5_log_dt[li], s5_b_re[li], s5_b_im[li], s5_c_re[li], s5_c_im[li])
            w_glu = s5_w_glu[li].astype(BF16)
            w_out = w_out_odd[li].astype(BF16)
            w_o1, w_o2 = w_out[:S5_CH], w_out[S5_CH:]
            R = FOX_H // FOX_G
            z = rms_matmul(yp, ln_mix_odd[li][None], w, TM, O_N // 3)
            u, q, k, v, lf, cum = odd_post(z, g_q_fox[li][None], g_k_fox[li][None], bf, 256, BF16, B, True)
            y5, sre, sim = s5_prompt(u.reshape(B, S, S5_CH), *s5p)
            o_s5 = s5_glu(y5.reshape(MP, S5_CH), u, s5_d[li][None], w_glu, s5_b_glu[li][None], TM)
            cum3 = cum[:, :FOX_H].reshape(B, S, FOX_G, R)
            cq = jnp.pad(cum3.transpose(0, 2, 1, 3), ((0, 0), (0, 0), (0, 0), (0, 128 - R)))
            ck = jnp.pad(cum3.transpose(0, 2, 3, 1), ((0, 0), (0, 0), (0, 8 - R), (0, 0)))
            r3 = lambda a: a.reshape(B, S, a.shape[1])
            o_fox = flash_attention(r3(q), r3(k), r3(v), G=FOX_G, R=R, dq=FOX_D, dv=FOX_D, T=TA, TS=TS, cq=cq, ck=ck)
            yp = out_proj(yp, o_s5, o_fox.reshape(MP, -1), w_o1, w_o2, TM, 1024)
            outs.setdefault('fk_p', []).append(k.reshape(B, S, FOX_G, FOX_D))
            outs.setdefault('fv_p', []).append(v.reshape(B, S, FOX_G, FOX_D))
            outs.setdefault('fl_p', []).append(lf[:, :FOX_H].reshape(B, S, FOX_H))
            outs.setdefault('sre_p', []).append(sre)
            outs.setdefault('sim_p', []).append(sim)
            z = rms_matmul(ys, ln_mix_odd[li][None], w, N, O_N // 3)
            u, q, k, v, lf, _ = odd_post(z, g_q_fox[li][None], g_k_fox[li][None], bf, N, F32, 1, False)
            y5, sre, sim = s5_sample(u, state_s5_re[li], state_s5_im[li], *s5p)
            o_s5 = s5_glu(y5, u, s5_d[li][None], w_glu, s5_b_glu[li][None], N)
            sfx = fox_suffix(page_table, jnp.swapaxes(cache_fox_logf, 2, 3), li)
            o_fox = gqa_decode(page_table, q.reshape(N, FOX_H, FOX_D), k.reshape(N, FOX_G, FOX_D),
                               v.reshape(N, FOX_G, FOX_D), sfx, lf[:, :FOX_H, None],
                               cache_fox_k, cache_fox_v, li, PPC, 'forget')
            ys = out_proj(ys, o_s5, o_fox.reshape(N, -1), w_o1, w_o2, N, 1024)
            outs.setdefault('fk_s', []).append(k.reshape(N, 1, FOX_G, FOX_D))
            outs.setdefault('fv_s', []).append(v.reshape(N, 1, FOX_G, FOX_D))
            outs.setdefault('fl_s', []).append(lf[:, :FOX_H].reshape(N, 1, FOX_H))
            outs.setdefault('sre_s', []).append(sre)
            outs.setdefault('sim_s', []).append(sim)
        wu = w_up[layer].astype(BF16)
        wd = w_down[layer].astype(BF16)
        yp = mlp(yp, ln_mlp[layer][None], wu, wd, TM, 512)
        ys = mlp(ys, ln_mlp[layer][None], wu, wd, N, 512)
    order = ('lat_p', 'lat_s', 'kr_p', 'kr_s', 'dk_p', 'dk_s', 'dv_p', 'dv_s', 'di_p', 'di_s',
             'fk_p', 'fk_s', 'fv_p', 'fv_s', 'fl_p', 'fl_s', 'sre_p', 'sre_s', 'sim_p', 'sim_s')
    return (yp.reshape(B, S, D), ys.reshape(N, 1, D)) + tuple(jnp.stack(outs[name]) for name in order)
```

```python
import functools
import math

import numpy as np
import jax
import jax.numpy as jnp
from jax import lax
from jax.experimental import pallas as pl
from jax.experimental.pallas import tpu as pltpu

F32 = jnp.float32
BF16 = jnp.bfloat16
I32 = jnp.int32

LANES = 128
VMEM_LIMIT_BYTES = 56 * 1024 * 1024

D_MODEL = 2048
PAGE = 128
EPS = 1e-6
ROPE_BASE = 10000.0
MLA_H, MLA_NOPE, MLA_ROPE, MLA_V, MLA_LAT = 8, 128, 64, 128, 256
MLA_QK = MLA_NOPE + MLA_ROPE
MLA_PAD = 256
DSA_H, DSA_G, DSA_D = 8, 2, 128
IDX_H, IDX_D = 4, 64
DSA_TOPK = 256
S5_CH, S5_GRP, S5_G, S5_P = 1024, 16, 64, 64
S5_L = 16
FOX_H, FOX_G, FOX_D = 8, 2, 128
D_FF = 4 * D_MODEL
NEG = -1e30
INT_MIN = -(2 ** 31)
LOG2E = math.log2(math.e)

E_Q, E_LAT, E_QD, E_KD, E_VD, E_QI, E_KK, E_WI, E_N = 0, 2048, 2304, 3328, 3584, 3840, 4096, 4224, 4352
O_U, O_Q, O_K, O_V, O_F, O_N = 0, 1024, 2048, 2304, 2560, 2688


def _cp(sem):
    return pltpu.CompilerParams(dimension_semantics=sem, vmem_limit_bytes=VMEM_LIMIT_BYTES)


def _nt(a, b, precision=None):
    return lax.dot_general(a, b, (((1,), (1,)), ((), ())), preferred_element_type=F32, precision=precision)


def _rms_mm_kernel(x_ref, g_ref, w_ref, o_ref, h_ref):
    @pl.when(pl.program_id(1) == 0)
    def _():
        x = x_ref[...]
        ms = jnp.mean(x * x, axis=-1, keepdims=True)
        h_ref[...] = ((x * lax.rsqrt(ms + EPS)) * g_ref[...]).astype(BF16)

    o_ref[...] = jnp.dot(h_ref[...], w_ref[...], preferred_element_type=F32)


def rms_matmul(x, g, w, tm, tn):
    M, K = x.shape
    N = w.shape[1]
    return pl.pallas_call(
        _rms_mm_kernel, grid=(M // tm, N // tn),
        in_specs=[pl.BlockSpec((tm, K), lambda i, j: (i, 0)),
                  pl.BlockSpec((1, K), lambda i, j: (0, 0)),
                  pl.BlockSpec((K, tn), lambda i, j: (0, j))],
        out_specs=pl.BlockSpec((tm, tn), lambda i, j: (i, j)),
        out_shape=jax.ShapeDtypeStruct((M, N), F32),
        scratch_shapes=[pltpu.VMEM((tm, K), BF16)],
        compiler_params=_cp(("parallel", "arbitrary")), name="rms_matmul")(x, g, w)


def _mm_kernel(a_ref, w_ref, o_ref):
    o_ref[...] = jnp.dot(a_ref[...].astype(BF16), w_ref[...], preferred_element_type=F32).astype(o_ref.dtype)


def matmul(a, w, out_dtype, tm, tn):
    M, K = a.shape
    N = w.shape[1]
    return pl.pallas_call(
        _mm_kernel, grid=(M // tm, N // tn),
        in_specs=[pl.BlockSpec((tm, K), lambda i, j: (i, 0)), pl.BlockSpec((K, tn), lambda i, j: (0, j))],
        out_specs=pl.BlockSpec((tm, tn), lambda i, j: (i, j)),
        out_shape=jax.ShapeDtypeStruct((M, N), out_dtype),
        compiler_params=_cp(("parallel", "arbitrary")), name="matmul")(a, w)


def _out_proj_kernel(x_ref, a1_ref, a2_ref, w1_ref, w2_ref, o_ref):
    acc = jnp.dot(a1_ref[...].astype(BF16), w1_ref[...], preferred_element_type=F32)
    acc = acc + jnp.dot(a2_ref[...].astype(BF16), w2_ref[...], preferred_element_type=F32)
    o_ref[...] = x_ref[...] + acc


def out_proj(x, a1, a2, w1, w2, tm, tn):
    M, N = x.shape
    K1, K2 = a1.shape[1], a2.shape[1]
    return pl.pallas_call(
        _out_proj_kernel, grid=(M // tm, N // tn),
        in_specs=[pl.BlockSpec((tm, tn), lambda i, j: (i, j)),
                  pl.BlockSpec((tm, K1), lambda i, j: (i, 0)),
                  pl.BlockSpec((tm, K2), lambda i, j: (i, 0)),
                  pl.BlockSpec((K1, tn), lambda i, j: (0, j)),
                  pl.BlockSpec((K2, tn), lambda i, j: (0, j))],
        out_specs=pl.BlockSpec((tm, tn), lambda i, j: (i, j)),
        out_shape=jax.ShapeDtypeStruct((M, N), F32),
        compiler_params=_cp(("parallel", "arbitrary")), name="out_proj")(x, a1, a2, w1, w2)


def _mlp_kernel(x_ref, g_ref, wu_ref, wd_ref, o_ref, h_ref, acc_ref):
    f = pl.program_id(1)

    @pl.when(f == 0)
    def _():
        x = x_ref[...]
        ms = jnp.mean(x * x, axis=-1, keepdims=True)
        h_ref[...] = ((x * lax.rsqrt(ms + EPS)) * g_ref[...]).astype(BF16)
        acc_ref[...] = jnp.zeros_like(acc_ref)

    up = jnp.dot(h_ref[...], wu_ref[...], preferred_element_type=F32)
    act = jnp.square(jnp.maximum(up, 0.0)).astype(BF16)
    acc_ref[...] += jnp.dot(act, wd_ref[...], preferred_element_type=F32)

    @pl.when(f == pl.num_programs(1) - 1)
    def _():
        o_ref[...] = x_ref[...] + acc_ref[...]


def mlp(x, g, w_up, w_down, tm, tf):
    M, D = x.shape
    FF = w_up.shape[1]
    return pl.pallas_call(
        _mlp_kernel, grid=(M // tm, FF // tf),
        in_specs=[pl.BlockSpec((tm, D), lambda i, f: (i, 0)),
                  pl.BlockSpec((1, D), lambda i, f: (0, 0)),
                  pl.BlockSpec((D, tf), lambda i, f: (0, f)),
                  pl.BlockSpec((tf, D), lambda i, f: (f, 0))],
        out_specs=pl.BlockSpec((tm, D), lambda i, f: (i, 0)),
        out_shape=jax.ShapeDtypeStruct((M, D), F32),
        scratch_shapes=[pltpu.VMEM((tm, D), BF16), pltpu.VMEM((tm, D), F32)],
        compiler_params=_cp(("parallel", "arbitrary")), name="mlp")(x, g, w_up, w_down)


def _rope128(r, c, s):
    lane = lax.broadcasted_iota(I32, r.shape, 1)
    partner = jnp.where(lane < 32, pltpu.roll(r, 96, 1), pltpu.roll(r, 32, 1))
    return r * c + partner * s


def _even_post_kernel(z_ref, cos_ref, sin_ref, gq_ref, gl_ref, gkr_ref, gqd_ref, gkd_ref,
                      qm_ref, lat_ref, kr_ref, lk_ref, qd_ref, kd_ref, vd_ref, qi_ref, ki_ref, wi_ref):
    c = cos_ref[...]
    s = sin_ref[...]
    mla_scale = MLA_QK ** -0.5 * LOG2E
    dsa_scale = DSA_D ** -0.5 * LOG2E
    gq = gq_ref[...]
    for h in range(MLA_H):
        x = z_ref[:, E_Q + h * MLA_PAD:E_Q + (h + 1) * MLA_PAD]
        ms = jnp.sum(x * x, axis=-1, keepdims=True) * (1.0 / MLA_QK)
        y = (x * lax.rsqrt(ms + EPS)) * gq
        qm_ref[:, h * MLA_PAD:h * MLA_PAD + 128] = (y[:, :128] * mla_scale).astype(qm_ref.dtype)
        qm_ref[:, h * MLA_PAD + 128:(h + 1) * MLA_PAD] = (_rope128(y[:, 128:], c, s) * mla_scale).astype(qm_ref.dtype)
    x = z_ref[:, E_LAT:E_LAT + MLA_LAT]
    ms = jnp.mean(x * x, axis=-1, keepdims=True)
    lat = (x * lax.rsqrt(ms + EPS)) * gl_ref[...]
    lat_ref[...] = lat
    zk = z_ref[:, E_KK:E_KK + 128]
    lane = lax.broadcasted_iota(I32, zk.shape, 1)
    kr_in = jnp.where(lane < MLA_ROPE, zk, 0.0)
    ms = jnp.sum(kr_in * kr_in, axis=-1, keepdims=True) * (1.0 / MLA_ROPE)
    kr = _rope128((kr_in * lax.rsqrt(ms + EPS)) * gkr_ref[...], c, s)
    kr_ref[...] = kr[:, :MLA_ROPE]
    ki_ref[...] = zk[:, MLA_ROPE:]
    lk_ref[:, :MLA_LAT] = lat.astype(BF16)
    lk_ref[:, MLA_LAT:] = kr.astype(BF16)
    for h in range(DSA_H):
        x = z_ref[:, E_QD + h * DSA_D:E_QD + (h + 1) * DSA_D]
        ms = jnp.mean(x * x, axis=-1, keepdims=True)
        qd_ref[:, h * DSA_D:(h + 1) * DSA_D] = ((x * lax.rsqrt(ms + EPS)) * gqd_ref[...] * dsa_scale).astype(qd_ref.dtype)
    for h in range(DSA_G):
        x = z_ref[:, E_KD + h * DSA_D:E_KD + (h + 1) * DSA_D]
        ms = jnp.mean(x * x, axis=-1, keepdims=True)
        kd_ref[:, h * DSA_D:(h + 1) * DSA_D] = (x * lax.rsqrt(ms + EPS)) * gkd_ref[...]
    vd_ref[...] = z_ref[:, E_VD:E_VD + DSA_G * DSA_D]
    qi_ref[...] = z_ref[:, E_QI:E_QI + IDX_H * IDX_D].astype(qi_ref.dtype)
    wi_ref[...] = z_ref[:, E_WI:E_WI + 128]


def even_post(z, cos, sin, gq, gl, gkr, gqd, gkd, tm, q_dtype, n_pos_blocks):
    M = z.shape[0]
    row = lambda i: (i, 0)
    pos = lambda i: (i % n_pos_blocks, 0)
    fix = lambda i: (0, 0)
    widths = [(MLA_H * MLA_PAD, q_dtype), (MLA_LAT, F32), (MLA_ROPE, F32), (MLA_LAT + 128, BF16),
              (DSA_H * DSA_D, q_dtype), (DSA_G * DSA_D, F32), (DSA_G * DSA_D, F32),
              (IDX_H * IDX_D, q_dtype), (IDX_D, F32), (128, F32)]
    return pl.pallas_call(
        _even_post_kernel, grid=(M // tm,),
        in_specs=[pl.BlockSpec((tm, E_N), row), pl.BlockSpec((tm, 128), pos), pl.BlockSpec((tm, 128), pos),
                  pl.BlockSpec((1, MLA_PAD), fix), pl.BlockSpec((1, MLA_LAT), fix), pl.BlockSpec((1, 128), fix),
                  pl.BlockSpec((1, DSA_D), fix), pl.BlockSpec((1, DSA_D), fix)],
        out_specs=[pl.BlockSpec((tm, w), row) for w, _ in widths],
        out_shape=[jax.ShapeDtypeStruct((M, w), dt) for w, dt in widths],
        compiler_params=_cp(("parallel",)), name="even_post")(z, cos, sin, gq, gl, gkr, gqd, gkd)


def _log_sigmoid(x):
    return -(jnp.maximum(-x, 0.0) + jnp.log1p(jnp.exp(-jnp.abs(x))))


def _odd_post_kernel(z_ref, gq_ref, gk_ref, bf_ref, u_ref, q_ref, k_ref, v_ref, lf_ref, cum_ref, carry_ref, *, cumulative):
    scale = FOX_D ** -0.5 * LOG2E
    u_ref[...] = z_ref[:, O_U:O_U + S5_CH]
    for h in range(FOX_H):
        x = z_ref[:, O_Q + h * FOX_D:O_Q + (h + 1) * FOX_D]
        ms = jnp.mean(x * x, axis=-1, keepdims=True)
        q_ref[:, h * FOX_D:(h + 1) * FOX_D] = ((x * lax.rsqrt(ms + EPS)) * gq_ref[...] * scale).astype(q_ref.dtype)
    for h in range(FOX_G):
        x = z_ref[:, O_K + h * FOX_D:O_K + (h + 1) * FOX_D]
        ms = jnp.mean(x * x, axis=-1, keepdims=True)
        k_ref[:, h * FOX_D:(h + 1) * FOX_D] = (x * lax.rsqrt(ms + EPS)) * gk_ref[...]
    v_ref[...] = z_ref[:, O_V:O_V + FOX_G * FOX_D]
    zf = z_ref[:, O_F:O_F + 128]
    lane = lax.broadcasted_iota(I32, zf.shape, 1)
    lf = jnp.where(lane < FOX_H, _log_sigmoid(zf + bf_ref[...]), 0.0)
    lf_ref[...] = lf
    if cumulative:
        tm = zf.shape[0]

        @pl.when(pl.program_id(1) == 0)
        def _():
            carry_ref[...] = jnp.zeros_like(carry_ref)

        r = lax.broadcasted_iota(I32, (tm, tm), 0)
        cidx = lax.broadcasted_iota(I32, (tm, tm), 1)
        tri = jnp.where(cidx <= r, 1.0, 0.0).astype(F32)
        cum = jnp.dot(tri, lf, preferred_element_type=F32, precision=lax.Precision.HIGHEST) + carry_ref[...]
        cum_ref[...] = cum * LOG2E
        carry_ref[...] = cum[tm - 1:tm, :]
    else:
        cum_ref[...] = lf


def odd_post(z, gq, gk, bf, tm, q_dtype, n_batch, cumulative):
    M = z.shape[0]
    nb = M // tm // n_batch
    row = lambda b, i: (b * nb + i, 0)
    fix = lambda b, i: (0, 0)
    widths = [(S5_CH, F32), (FOX_H * FOX_D, q_dtype), (FOX_G * FOX_D, F32), (FOX_G * FOX_D, F32), (128, F32), (128, F32)]
    return pl.pallas_call(
        functools.partial(_odd_post_kernel, cumulative=cumulative), grid=(n_batch, nb),
        in_specs=[pl.BlockSpec((tm, O_N), row), pl.BlockSpec((1, FOX_D), fix), pl.BlockSpec((1, FOX_D), fix),
                  pl.BlockSpec((1, 128), fix)],
        out_specs=[pl.BlockSpec((tm, w), row) for w, _ in widths],
        out_shape=[jax.ShapeDtypeStruct((M, w), dt) for w, dt in widths],
        scratch_shapes=[pltpu.VMEM((1, 128), F32)],
        compiler_params=_cp(("arbitrary", "arbitrary")), name="odd_post")(z, gq, gk, bf)


def _flash_kernel(qi_ref, ki_ref, *refs, R, dq, dv, T, TS, shared_kv, has_bias, has_mask):
    q_ref, k_ref, v_ref = refs[:3]
    pos = 3
    cq_ref = ck_ref = mask_ref = None
    if has_bias:
        cq_ref, ck_ref = refs[pos:pos + 2]
        pos += 2
    if has_mask:
        mask_ref = refs[pos]
        pos += 1
    o_ref, m_ref, l_ref, acc_ref = refs[pos:pos + 4]
    t = pl.program_id(2)
    qi = qi_ref[t]
    ki = ki_ref[t]

    @pl.when(ki == 0)
    def _():
        m_ref[...] = jnp.full_like(m_ref, NEG)
        l_ref[...] = jnp.zeros_like(l_ref)
        acc_ref[...] = jnp.zeros_like(acc_ref)

    def step(diag):
        for r in range(R):
            kr = 0 if shared_kv else r
            if r == 0 or not shared_kv:
                k = k_ref[0, :, kr * dq:(kr + 1) * dq].astype(BF16)
                v = v_ref[0, :, kr * dv:(kr + 1) * dv].astype(BF16)
            for qs in range(T // TS):
                rows = slice(qs * TS, (qs + 1) * TS)
                ncol = (qs + 1) * TS if diag else T
                s = _nt(q_ref[0, rows, r * dq:(r + 1) * dq], k[:ncol])
                if has_bias:
                    s = s + (cq_ref[0, 0, rows, r:r + 1] - ck_ref[0, 0, r:r + 1, :ncol])
                if has_mask:
                    mask = mask_ref[0, :, 0].reshape(T, T)
                    s = s + mask[rows, :ncol].astype(F32)
                elif diag:
                    row = qs * TS + lax.broadcasted_iota(I32, (TS, ncol), 0)
                    col = lax.broadcasted_iota(I32, (TS, ncol), 1)
                    s = jnp.where(col <= row, s, NEG)
                m_prev = m_ref[r, rows]
                m_new = jnp.maximum(m_prev, jnp.max(s, axis=1, keepdims=True))
                alpha = jnp.exp2(m_prev - m_new)
                ps = [jnp.exp2(s[:, c * LANES:(c + 1) * LANES] - m_new) for c in range(ncol // LANES)]
                l_ref[r, rows] = alpha * l_ref[r, rows] + functools.reduce(lambda a, b: a + b, ps)
                p = jnp.concatenate(ps, axis=1).astype(BF16)
                acc_ref[r, rows] = alpha * acc_ref[r, rows] + jnp.dot(p, v[:ncol], preferred_element_type=F32)
                m_ref[r, rows] = m_new

    @pl.when(ki < qi)
    def _():
        step(False)

    @pl.when(ki == qi)
    def _():
        step(True)
        for r in range(R):
            l_tot = jnp.sum(l_ref[r], axis=1, keepdims=True)
            o_ref[0, :, r * dv:(r + 1) * dv] = (acc_ref[r] / l_tot).astype(o_ref.dtype)


def flash_attention(q, k, v, *, G, R, dq, dv, T, TS, shared_kv=True, cq=None, ck=None, mask=None, k_block0=0, v_block0=0):
    assert dv == LANES
    B, S, _ = q.shape
    n = S // T
    pairs = [(i, j) for i in range(n) for j in range(i + 1)]
    qi_list = jnp.asarray([p[0] for p in pairs], I32)
    ki_list = jnp.asarray([p[1] for p in pairs], I32)
    kw = 1 if shared_kv else R
    qmap = lambda b, g, t, qi, ki: (b, qi[t], g)
    kmap = lambda b, g, t, qi, ki: (b, ki[t], k_block0 + g)
    vmap = lambda b, g, t, qi, ki: (b, ki[t], v_block0 + g)
    in_specs = [pl.BlockSpec((1, T, R * dq), qmap), pl.BlockSpec((1, T, kw * dq), kmap), pl.BlockSpec((1, T, kw * dv), vmap)]
    args = [q, k, v]
    if cq is not None:
        in_specs += [pl.BlockSpec((1, 1, T, 128), lambda b, g, t, qi, ki: (b, g, qi[t], 0)),
                     pl.BlockSpec((1, 1, 8, T), lambda b, g, t, qi, ki: (b, g, 0, ki[t]))]
        args += [cq, ck]
    if mask is not None:
        tq = mask.shape[3]
        assert mask.shape[4] == T and T % tq == 0
        in_specs += [pl.BlockSpec((1, T // tq, 1, tq, T), lambda b, g, t, qi, ki: (b, qi[t], ki[t], 0, 0))]
        args += [mask]
    kern = functools.partial(_flash_kernel, R=R, dq=dq, dv=dv, T=T, TS=TS, shared_kv=shared_kv,
                             has_bias=cq is not None, has_mask=mask is not None)
    grid_spec = pltpu.PrefetchScalarGridSpec(
        num_scalar_prefetch=2, grid=(B, G, len(pairs)), in_specs=in_specs,
        out_specs=pl.BlockSpec((1, T, R * dv), qmap),
        scratch_shapes=[pltpu.VMEM((R, T, LANES), F32), pltpu.VMEM((R, T, LANES), F32), pltpu.VMEM((R, T, dv), F32)])
    return pl.pallas_call(
        kern, grid_spec=grid_spec, out_shape=jax.ShapeDtypeStruct((B, S, G * R * dv), BF16),
        compiler_params=_cp(("parallel", "parallel", "arbitrary")), name="flash_attention")(qi_list, ki_list, *args)


def _sortable_key(score):
    bits = lax.bitcast_convert_type(score, I32)
    key = bits ^ ((bits >> 31) & jnp.int32(0x7FFFFFFF))
    return jnp.where(key == -1, 0, key)


def _count(mask):
    return jnp.sum(jnp.where(mask, 1.0, 0.0), axis=1, keepdims=True)


def _topk_bias(key_ref, valid, k, j_ref, n_idx_bits):
    rows, cols = key_ref.shape
    kf = float(k)

    def kth(i, t):
        cand = t + jnp.left_shift(jnp.int32(1), 31 - i)
        cnt = _count(key_ref[...] >= cand)
        return jnp.where(cnt >= kf, cand, t)

    thr = lax.fori_loop(0, 32, kth, jnp.full((rows, 1), INT_MIN, I32))
    key = key_ref[...]
    gt = key > thr
    eq = (key == thr) & valid
    need = kf - _count(gt)
    n_eq = _count(eq)
    col = lax.broadcasted_iota(I32, (rows, cols), 1)
    j_ref[...] = jnp.full((rows, 1), cols, I32)
    overfull = jnp.max(jnp.where(n_eq > need, 1.0, 0.0))

    @pl.when(overfull > 0.0)
    def _():
        def cut(i, j0):
            cand = j0 + jnp.left_shift(jnp.int32(1), n_idx_bits - 1 - i)
            cnt = _count((key_ref[...] == thr) & valid & (col < cand))
            return jnp.where(cnt < need, cand, j0)

        j_ref[...] = lax.fori_loop(0, n_idx_bits, cut, jnp.zeros((rows, 1), I32))

    sel = valid & (gt | (eq & (col <= j_ref[...])))
    return jnp.where(sel, 0.0, NEG)


def _dsa_mask_t_kernel(qi_ref, wt_ref, ki_ref, bias_ref, key_ref, tie_ref, j_ref, *, tq, cw, S):
    qb = pl.program_id(1)
    n_all = S // cw
    nc = (qb * tq + tq - 1) // cw + 1
    q = qi_ref[0]
    wt = wt_ref[0]
    kf = float(DSA_TOPK)
    krow = lax.broadcasted_iota(I32, (cw, tq), 0)
    qidx = qb * tq + lax.broadcasted_iota(I32, (cw, tq), 1)

    def score_chunk(c, carry):
        kb = ki_ref[0, pl.ds(pl.multiple_of(c * cw, cw), cw), :].astype(BF16)
        sc = jnp.zeros((cw, tq), F32)
        for h in range(IDX_H):
            sc = sc + wt[h:h + 1, :] * jnp.maximum(_nt(kb, q[:, h * IDX_D:(h + 1) * IDX_D]), 0.0)
        key_ref[c] = jnp.where(c * cw + krow <= qidx, _sortable_key(sc), INT_MIN)
        return carry

    lax.fori_loop(0, nc, score_chunk, 0)

    def count(ref, pred):
        def body(c, acc):
            hit = jnp.where(pred(ref[c]), 1.0, 0.0)
            return acc + jnp.sum(hit.reshape(cw // 64, 64, tq), axis=0)
        return jnp.sum(lax.fori_loop(0, nc, body, jnp.zeros((64, tq), F32)), axis=0, keepdims=True)

    def kth(i, t):
        cand = t + jnp.left_shift(jnp.int32(1), 31 - i)
        return jnp.where(count(key_ref, lambda k: k >= cand) >= kf, cand, t)

    thr = lax.fori_loop(0, 32, kth, jnp.full((1, tq), INT_MIN, I32))
    need = kf - count(key_ref, lambda k: k > thr)

    def tie_chunk(c, carry):
        kidx = c * cw + krow
        tie_ref[c] = jnp.where((key_ref[c] == thr) & (kidx <= qidx), kidx, 2 * S)
        return carry

    lax.fori_loop(0, nc, tie_chunk, 0)
    n_eq = count(tie_ref, lambda t: t < 2 * S)
    j_ref[...] = jnp.full((1, tq), S, I32)
    overfull = jnp.max(jnp.where(n_eq > need, 1.0, 0.0))

    @pl.when(overfull > 0.0)
    def _():
        def cut(i, j0):
            cand = j0 + jnp.left_shift(jnp.int32(1), int(math.log2(S)) - 1 - i)
            return jnp.where(count(tie_ref, lambda t: t < cand) < need, cand, j0)

        j_ref[...] = lax.fori_loop(0, int(math.log2(S)), cut, jnp.zeros((1, tq), I32))

    jcut = j_ref[...]

    def write_chunk(c, carry):
        sel = (key_ref[c] > thr) | (tie_ref[c] <= jcut)
        bias_ref[0, 0, c] = jnp.where(sel, 0.0, NEG).T.astype(BF16)
        return carry

    lax.fori_loop(0, nc, write_chunk, 0)

    def fill_chunk(c, carry):
        bias_ref[0, 0, c] = jnp.full((tq, cw), NEG, BF16)
        return carry

    lax.fori_loop(nc, n_all, fill_chunk, 0)


def dsa_prompt_mask_t(q_idx, w_idx_t, k_idx, tq, cw):
    B, S, _ = q_idx.shape
    return pl.pallas_call(
        functools.partial(_dsa_mask_t_kernel, tq=tq, cw=cw, S=S), grid=(B, S // tq),
        in_specs=[pl.BlockSpec((1, tq, IDX_H * IDX_D), lambda b, i: (b, i, 0)),
                  pl.BlockSpec((1, 8, tq), lambda b, i: (b, 0, i)),
                  pl.BlockSpec((1, S, IDX_D), lambda b, i: (b, 0, 0))],
        out_specs=pl.BlockSpec((1, 1, S // cw, tq, cw), lambda b, i: (b, i, 0, 0, 0)),
        out_shape=jax.ShapeDtypeStruct((B, S // tq, S // cw, tq, cw), BF16),
        scratch_shapes=[pltpu.VMEM((S // cw, cw, tq), I32), pltpu.VMEM((S // cw, cw, tq), I32), pltpu.VMEM((1, tq), I32)],
        compiler_params=_cp(("parallel", "parallel")), name="dsa_prompt_mask")(q_idx, w_idx_t, k_idx)


def _s5_params(lam_re, lam_im, log_dt, b_re, b_im, c_re, c_im):
    dt = jnp.exp(log_dt)[:, None]
    def apow(n):
        mag = jnp.exp(lam_re * dt * n)
        return mag * jnp.cos(lam_im * dt * n), mag * jnp.sin(lam_im * dt * n)
    a_re, a_im = apow(1.0)
    den = lam_re * lam_re + lam_im * lam_im
    x, y = a_re - 1.0, a_im
    co_re = (x * lam_re + y * lam_im) / den
    co_im = (y * lam_re - x * lam_im) / den
    bb_re = co_re[..., None] * b_re - co_im[..., None] * b_im
    bb_im = co_re[..., None] * b_im + co_im[..., None] * b_re
    return a_re, a_im, bb_re, bb_im, apow


def _s5_chunk_operators(lam_re, lam_im, log_dt, b_re, b_im, c_re, c_im):
    a_re, a_im, bb_re, bb_im, apow = _s5_params(lam_re, lam_im, log_dt, b_re, b_im, c_re, c_im)
    L = S5_L
    taus = jnp.arange(L + 1, dtype=F32)
    pw = [apow(float(t)) for t in range(L + 1)]
    pw_re = jnp.stack([p[0] for p in pw])
    pw_im = jnp.stack([p[1] for p in pw])
    ab_re = pw_re[..., None] * bb_re[None] - pw_im[..., None] * bb_im[None]
    ab_im = pw_re[..., None] * bb_im[None] + pw_im[..., None] * bb_re[None]
    hp = lax.Precision.HIGHEST
    kern = (jnp.einsum('gcp,tgpd->tgcd', c_re, ab_re[:L], precision=hp)
            - jnp.einsum('gcp,tgpd->tgcd', c_im, ab_im[:L], precision=hp))
    t_out = np.arange(L)[None, :]
    s_in = np.arange(L)[:, None]
    lag = np.clip(t_out - s_in, 0, L - 1)
    m = kern[lag]
    m = jnp.where((t_out >= s_in)[:, :, None, None, None], m, 0.0)
    m = m.transpose(2, 0, 4, 1, 3).reshape(S5_G, L * S5_GRP, L * S5_GRP)
    w_re = ab_re[L - 1 - np.arange(L)]
    w_im = ab_im[L - 1 - np.arange(L)]
    w = jnp.concatenate([w_re, w_im], axis=2)
    w = w.transpose(1, 0, 3, 2).reshape(S5_G, L * S5_GRP, 2 * S5_P)
    ca_re = c_re[None] * pw_re[1:, :, None, :] - c_im[None] * pw_im[1:, :, None, :]
    ca_im = c_re[None] * pw_im[1:, :, None, :] + c_im[None] * pw_re[1:, :, None, :]
    v = jnp.concatenate([ca_re, -ca_im], axis=3)
    v = v.transpose(1, 3, 0, 2).reshape(S5_G, 2 * S5_P, L * S5_GRP)
    return m.astype(BF16), w.astype(BF16), v.astype(BF16), apow


def _s5_chunk_kernel(u_ref, k_ref, w_ref, v_ref, apr_ref, api_ref, y_ref, xr_ref, xi_ref, *, n_chunks):
    L = S5_L
    W = S5_GB * S5_P
    ua = jnp.concatenate([u_ref[pl.ds(s, n_chunks, stride=L), :] for s in range(L)], axis=1).astype(BF16)
    y = jnp.dot(ua, k_ref[0], preferred_element_type=F32)
    d = jnp.dot(ua, w_ref[0], preferred_element_type=F32)
    xr = d[:, :W]
    xi = d[:, W:]
    cidx = lax.broadcasted_iota(I32, (n_chunks, W), 0)
    for k in range(int(math.log2(n_chunks))):
        sh = 1 << k
        ar = apr_ref[0, k:k + 1, :]
        ai = api_ref[0, k:k + 1, :]
        keep = cidx >= sh
        sr = jnp.where(keep, pltpu.roll(xr, sh, 0), 0.0)
        si = jnp.where(keep, pltpu.roll(xi, sh, 0), 0.0)
        xr, xi = xr + ar * sr - ai * si, xi + ar * si + ai * sr
    keep = cidx >= 1
    xin = jnp.concatenate([jnp.where(keep, pltpu.roll(xr, 1, 0), 0.0),
                           jnp.where(keep, pltpu.roll(xi, 1, 0), 0.0)], axis=1).astype(BF16)
    y = y + jnp.dot(xin, v_ref[0], preferred_element_type=F32)
    for s in range(L):
        y_ref[pl.ds(s, n_chunks, stride=L), :] = y[:, s * LANES:(s + 1) * LANES]
    xr_ref[0] = xr[n_chunks - 1:n_chunks, :]
    xi_ref[0] = xi[n_chunks - 1:n_chunks, :]


def s5_prompt(u, lam_re, lam_im, log_dt, b_re, b_im, c_re, c_im):
    B, T, _ = u.shape
    L = S5_L
    nch = T // L
    GB = S5_GB
    nb = S5_G // GB
    m, w, v, apow = _s5_chunk_operators(lam_re, lam_im, log_dt, b_re, b_im, c_re, c_im)
    nlev = int(math.log2(nch))
    ap = [apow(float(L * (1 << k))) for k in range(nlev)]
    blk = lambda a: a.reshape(nb, GB, nlev, S5_P).transpose(0, 2, 1, 3).reshape(nb, nlev, GB * S5_P)
    apr = blk(jnp.stack([p[0] for p in ap], axis=1))
    api = blk(jnp.stack([p[1] for p in ap], axis=1))
    eye = jnp.eye(GB, dtype=BF16)
    kbig = jnp.einsum('jqsitc,qr->jsqitrc', m.reshape(nb, GB, L, S5_GRP, L, S5_GRP), eye).reshape(nb, L * LANES, L * LANES)
    wbig = jnp.einsum('jqsiep,qr->jsqierp', w.reshape(nb, GB, L, S5_GRP, 2, S5_P), eye).reshape(nb, L * LANES, 2 * GB * S5_P)
    vbig = jnp.einsum('jqeptc,qr->jeqptrc', v.reshape(nb, GB, 2, S5_P, L, S5_GRP), eye).reshape(nb, 2 * GB * S5_P, L * LANES)
    W = GB * S5_P
    wmap = lambda j, b: (j, 0, 0)
    y, xr, xi = pl.pallas_call(
        functools.partial(_s5_chunk_kernel, n_chunks=nch), grid=(nb, B),
        in_specs=[pl.BlockSpec((T, LANES), lambda j, b: (b, j)),
                  pl.BlockSpec((1, L * LANES, L * LANES), wmap), pl.BlockSpec((1, L * LANES, 2 * W), wmap),
                  pl.BlockSpec((1, 2 * W, L * LANES), wmap),
                  pl.BlockSpec((1, nlev, W), wmap), pl.BlockSpec((1, nlev, W), wmap)],
        out_specs=[pl.BlockSpec((T, LANES), lambda j, b: (b, j)),
                   pl.BlockSpec((1, 1, W), lambda j, b: (b, 0, j)), pl.BlockSpec((1, 1, W), lambda j, b: (b, 0, j))],
        out_shape=[jax.ShapeDtypeStruct((B * T, S5_CH), F32),
                   jax.ShapeDtypeStruct((B, 1, S5_G * S5_P), F32), jax.ShapeDtypeStruct((B, 1, S5_G * S5_P), F32)],
        compiler_params=_cp(("parallel", "arbitrary")), name="s5_chunk_scan")(
            u.reshape(B * T, S5_CH), kbig, wbig, vbig, apr, api)
    return y, xr.reshape(B, S5_G, S5_P), xi.reshape(B, S5_G, S5_P)


def _gelu_tanh(x):
    return 0.5 * x * (1.0 + jnp.tanh(math.sqrt(2.0 / math.pi) * (x + 0.044715 * (x * x * x))))


def _s5_glu_kernel(y_ref, u_ref, d_ref, w_ref, b_ref, o_ref):
    h = _gelu_tanh(y_ref[...] + d_ref[...] * u_ref[...])
    gate = jnp.dot(h.astype(BF16), w_ref[...], preferred_element_type=F32) + b_ref[...]
    o_ref[...] = (h * (1.0 / (1.0 + jnp.exp(-gate)))).astype(o_ref.dtype)


def s5_glu(y, u, d, w_glu, b_glu, tm):
    M, C = y.shape
    row = lambda i: (i, 0)
    fix = lambda i: (0, 0)
    return pl.pallas_call(
        _s5_glu_kernel, grid=(M // tm,),
        in_specs=[pl.BlockSpec((tm, C), row), pl.BlockSpec((tm, C), row), pl.BlockSpec((1, C), fix),
                  pl.BlockSpec((C, C), fix), pl.BlockSpec((1, C), fix)],
        out_specs=pl.BlockSpec((tm, C), row),
        out_shape=jax.ShapeDtypeStruct((M, C), BF16),
        compiler_params=_cp(("parallel",)), name="s5_glu")(y, u, d, w_glu, b_glu)


S5_GB = 8


def _s5_step_kernel(u_ref, x0r_ref, x0i_ref, ar_ref, ai_ref, br_ref, bi_ref, cr_ref, ci_ref, y_ref, xr_ref, xi_ref):
    u = u_ref[...].astype(BF16)
    ar, ai = ar_ref[...], ai_ref[...]
    x0r, x0i = x0r_ref[...], x0i_ref[...]
    xr = ar * x0r - ai * x0i + jnp.dot(u, br_ref[0], preferred_element_type=F32)
    xi = ar * x0i + ai * x0r + jnp.dot(u, bi_ref[0], preferred_element_type=F32)
    xr_ref[...] = xr
    xi_ref[...] = xi
    y_ref[...] = (jnp.dot(xr.astype(BF16), cr_ref[0], preferred_element_type=F32)
                  - jnp.dot(xi.astype(BF16), ci_ref[0], preferred_element_type=F32))


def s5_sample(u, x0_re, x0_im, lam_re, lam_im, log_dt, b_re, b_im, c_re, c_im):
    N = u.shape[0]
    a_re, a_im, bb_re, bb_im, _ = _s5_params(lam_re, lam_im, log_dt, b_re, b_im, c_re, c_im)
    nb = S5_G // S5_GB
    eye = jnp.eye(S5_GB, dtype=F32)
    bd = lambda t: jnp.einsum('jqpc,qr->jqcrp', t.reshape(nb, S5_GB, S5_P, S5_GRP), eye).reshape(
        nb, S5_GB * S5_GRP, S5_GB * S5_P).astype(BF16)
    cd = lambda t: jnp.einsum('jqcp,qr->jrpqc', t.reshape(nb, S5_GB, S5_GRP, S5_P), eye).reshape(
        nb, S5_GB * S5_P, S5_GB * S5_GRP).astype(BF16)
    wu, ws = S5_GB * S5_GRP, S5_GB * S5_P
    col = lambda j: (0, j)
    blk = lambda j: (j, 0, 0)
    y, xr, xi = pl.pallas_call(
        _s5_step_kernel, grid=(nb,),
        in_specs=[pl.BlockSpec((N, wu), col), pl.BlockSpec((N, ws), col), pl.BlockSpec((N, ws), col),
                  pl.BlockSpec((1, ws), col), pl.BlockSpec((1, ws), col),
                  pl.BlockSpec((1, wu, ws), blk), pl.BlockSpec((1, wu, ws), blk),
                  pl.BlockSpec((1, ws, wu), blk), pl.BlockSpec((1, ws, wu), blk)],
        out_specs=[pl.BlockSpec((N, wu), col), pl.BlockSpec((N, ws), col), pl.BlockSpec((N, ws), col)],
        out_shape=[jax.ShapeDtypeStruct((N, S5_CH), F32), jax.ShapeDtypeStruct((N, S5_G * S5_P), F32),
                   jax.ShapeDtypeStruct((N, S5_G * S5_P), F32)],
        compiler_params=_cp(("parallel",)), name="s5_step")(
            u, x0_re.reshape(N, -1), x0_im.reshape(N, -1), a_re.reshape(1, -1), a_im.reshape(1, -1),
            bd(bb_re), bd(bb_im), cd(c_re), cd(c_im))
    return y, xr.reshape(N, S5_G, S5_P), xi.reshape(N, S5_G, S5_P)


def _fetch(pt_ref, n, n_chunks, pages_per_chunk, slot, copies, sem, wait):
    b = n // n_chunks
    c = n % n_chunks

    def body(p, carry):
        page = 0 if wait else pt_ref[b, c * pages_per_chunk + p]
        for src, dst in copies:
            cp = pltpu.make_async_copy(src(page), dst(slot, p), sem.at[slot])
            if wait:
                cp.wait()
            else:
                cp.start()
        return carry

    lax.fori_loop(0, pages_per_chunk, body, 0)


def _pipeline_pages(pt_ref, n_chunks, pages_per_chunk, copies, sem):
    n = pl.program_id(0)
    slot = n % 2

    @pl.when(n == 0)
    def _():
        _fetch(pt_ref, n, n_chunks, pages_per_chunk, slot, copies, sem, False)

    @pl.when(n + 1 < pl.num_programs(0))
    def _():
        _fetch(pt_ref, n + 1, n_chunks, pages_per_chunk, 1 - slot, copies, sem, False)

    _fetch(pt_ref, n, n_chunks, pages_per_chunk, slot, copies, sem, True)
    return slot


def _softmax_update(s, v, m_ref, l_ref, acc_ref, idx):
    m_prev = m_ref[idx]
    m_new = jnp.maximum(m_prev, jnp.max(s, axis=-1, keepdims=True))
    alpha = jnp.exp2(m_prev - m_new)
    p = jnp.exp2(s - m_new)
    l_ref[idx] = alpha * l_ref[idx] + jnp.sum(p, axis=-1, keepdims=True)
    acc_ref[idx] = alpha * acc_ref[idx] + jnp.dot(p.astype(BF16), v, preferred_element_type=F32)
    m_ref[idx] = m_new


def _softmax_finish(s_new, v_new, m_ref, l_ref, acc_ref, idx):
    m_prev = m_ref[idx]
    m_new = jnp.maximum(m_prev, s_new)
    alpha = jnp.exp2(m_prev - m_new)
    p = jnp.exp2(s_new - m_new)
    return (alpha * acc_ref[idx] + p * v_new) / (alpha * l_ref[idx] + p)


def _init_softmax(c, m_ref, l_ref, acc_ref):
    @pl.when(c == 0)
    def _():
        m_ref[...] = jnp.full_like(m_ref, NEG)
        l_ref[...] = jnp.zeros_like(l_ref)
        acc_ref[...] = jnp.zeros_like(acc_ref)


def _mla_decode_kernel(pt_ref, ql_ref, qr_ref, ln_ref, kn_ref, clat, ckr, o_ref,
                       latbuf, krbuf, sem, m_ref, l_ref, acc_ref, *, n_chunks, ppc, li):
    copies = [(lambda pg: clat.at[li, pg], lambda s, p: latbuf.at[s, p]),
              (lambda pg: ckr.at[li, pg], lambda s, p: krbuf.at[s, :, pl.ds(pl.multiple_of(p * PAGE, PAGE), PAGE)])]
    slot = _pipeline_pages(pt_ref, n_chunks, ppc, copies, sem)
    c = pl.program_id(0) % n_chunks
    _init_softmax(c, m_ref, l_ref, acc_ref)
    L = ppc * PAGE
    klat = latbuf[slot].reshape(L, MLA_LAT).astype(BF16)
    kkr_t = krbuf[slot].astype(BF16)
    ql = ql_ref[0]
    qr = qr_ref[0]
    s = _nt(ql.astype(BF16), klat) + jnp.dot(qr.astype(BF16), kkr_t, preferred_element_type=F32)
    _softmax_update(s, klat, m_ref, l_ref, acc_ref, 0)

    @pl.when(c == n_chunks - 1)
    def _():
        ln = ln_ref[0]
        s_new = jnp.sum(ql * ln, axis=-1, keepdims=True) + jnp.sum(qr * kn_ref[0], axis=-1, keepdims=True)
        o_ref[0] = _softmax_finish(s_new, ln, m_ref, l_ref, acc_ref, 0)


def mla_decode(page_table, q_lat, q_rope, lat_new, kr_new, cache_lat, cache_kr_t, li, ppc):
    N, n_pages = page_table.shape
    n_chunks = n_pages // ppc
    row = lambda n, pt: (n // n_chunks, 0, 0)
    grid_spec = pltpu.PrefetchScalarGridSpec(
        num_scalar_prefetch=1, grid=(N * n_chunks,),
        in_specs=[pl.BlockSpec((1, MLA_H, MLA_LAT), row), pl.BlockSpec((1, MLA_H, MLA_ROPE), row),
                  pl.BlockSpec((1, 1, MLA_LAT), row), pl.BlockSpec((1, 1, MLA_ROPE), row),
                  pl.BlockSpec(memory_space=pl.ANY), pl.BlockSpec(memory_space=pl.ANY)],
        out_specs=pl.BlockSpec((1, MLA_H, MLA_LAT), row),
        scratch_shapes=[pltpu.VMEM((2, ppc, PAGE, MLA_LAT), F32), pltpu.VMEM((2, MLA_ROPE, ppc * PAGE), F32),
                        pltpu.SemaphoreType.DMA((2,)),
                        pltpu.VMEM((1, MLA_H, 1), F32), pltpu.VMEM((1, MLA_H, 1), F32), pltpu.VMEM((1, MLA_H, MLA_LAT), F32)])
    return pl.pallas_call(
        functools.partial(_mla_decode_kernel, n_chunks=n_chunks, ppc=ppc, li=li), grid_spec=grid_spec,
        out_shape=jax.ShapeDtypeStruct((N, MLA_H, MLA_LAT), F32),
        compiler_params=_cp(("arbitrary",)), name="mla_decode")(page_table, q_lat, q_rope, lat_new, kr_new, cache_lat, cache_kr_t)


def _gqa_decode_kernel(*refs, n_chunks, ppc, li, G, R, D, bias_mode):
    pt_ref, q_ref, kn_ref, vn_ref = refs[:4]
    if bias_mode == 'mask':
        bp_ref, bn_ref = refs[4:6]
    else:
        sfx_ref, cn_ref = refs[4:6]
    ck, cv, o_ref, kbuf, vbuf, sem, m_ref, l_ref, acc_ref = refs[6:]
    copies = []
    for g in range(G):
        copies.append((lambda pg, g=g: ck.at[li, pg, :, g, :], lambda s, p, g=g: kbuf.at[s, g, p]))
        copies.append((lambda pg, g=g: cv.at[li, pg, :, g, :], lambda s, p, g=g: vbuf.at[s, g, p]))
    slot = _pipeline_pages(pt_ref, n_chunks, ppc, copies, sem)
    c = pl.program_id(0) % n_chunks
    _init_softmax(c, m_ref, l_ref, acc_ref)
    L = ppc * PAGE
    q = q_ref[0]
    for g in range(G):
        k = kbuf[slot, g].reshape(L, D).astype(BF16)
        v = vbuf[slot, g].reshape(L, D).astype(BF16)
        s = _nt(q[g * R:(g + 1) * R].astype(BF16), k)
        if bias_mode == 'mask':
            s = s + bp_ref[0]
        else:
            sfx = jnp.concatenate([sfx_ref[0, g * R:(g + 1) * R, p, :] for p in range(ppc)], axis=1)
            s = s + (sfx + cn_ref[0, g * R:(g + 1) * R, :]) * LOG2E
        _softmax_update(s, v, m_ref, l_ref, acc_ref, g)

    @pl.when(c == n_chunks - 1)
    def _():
        for g in range(G):
            qg = q[g * R:(g + 1) * R]
            s_new = jnp.sum(qg * kn_ref[0, g:g + 1, :], axis=-1, keepdims=True)
            if bias_mode == 'mask':
                s_new = s_new + bn_ref[0, :, 0:1]
            o_ref[0, g * R:(g + 1) * R, :] = _softmax_finish(s_new, vn_ref[0, g:g + 1, :], m_ref, l_ref, acc_ref, g)


def gqa_decode(page_table, q, k_new, v_new, bias_a, bias_b, cache_k, cache_v, li, ppc, bias_mode):
    N, n_pages = page_table.shape
    n_chunks = n_pages // ppc
    H, D = q.shape[1:]
    G = k_new.shape[1]
    R = H // G
    row = lambda n, pt: (n // n_chunks, 0, 0)
    if bias_mode == 'mask':
        bias_specs = [pl.BlockSpec((1, 1, ppc * PAGE), lambda n, pt: (n // n_chunks, 0, n % n_chunks)),
                      pl.BlockSpec((1, 1, 128), row)]
    else:
        bias_specs = [pl.BlockSpec((1, H, ppc, PAGE), lambda n, pt: (n // n_chunks, 0, n % n_chunks, 0)),
                      pl.BlockSpec((1, H, 1), row)]
    grid_spec = pltpu.PrefetchScalarGridSpec(
        num_scalar_prefetch=1, grid=(N * n_chunks,),
        in_specs=[pl.BlockSpec((1, H, D), row), pl.BlockSpec((1, G, D), row), pl.BlockSpec((1, G, D), row)]
        + bias_specs + [pl.BlockSpec(memory_space=pl.ANY), pl.BlockSpec(memory_space=pl.ANY)],
        out_specs=pl.BlockSpec((1, H, D), row),
        scratch_shapes=[pltpu.VMEM((2, G, ppc, PAGE, D), F32), pltpu.VMEM((2, G, ppc, PAGE, D), F32),
                        pltpu.SemaphoreType.DMA((2,)),
                        pltpu.VMEM((G, R, 1), F32), pltpu.VMEM((G, R, 1), F32), pltpu.VMEM((G, R, D), F32)])
    kern = functools.partial(_gqa_decode_kernel, n_chunks=n_chunks, ppc=ppc, li=li, G=G, R=R, D=D, bias_mode=bias_mode)
    return pl.pallas_call(
        kern, grid_spec=grid_spec, out_shape=jax.ShapeDtypeStruct((N, H, D), F32),
        compiler_params=_cp(("arbitrary",)), name="gqa_decode_" + bias_mode)(
            page_table, q, k_new, v_new, bias_a, bias_b, cache_k, cache_v)


def _dsa_score_kernel(pt_ref, qi_ref, wi_ref, cidx, o_ref, kbuf, sem, *, n_chunks, ppc, li):
    copies = [(lambda pg: cidx.at[li, pg], lambda s, p: kbuf.at[s, :, pl.ds(pl.multiple_of(p * PAGE, PAGE), PAGE)])]
    slot = _pipeline_pages(pt_ref, n_chunks, ppc, copies, sem)
    k_t = kbuf[slot].astype(BF16)
    d = jnp.maximum(jnp.dot(qi_ref[0].astype(BF16), k_t, preferred_element_type=F32), 0.0)
    o_ref[0] = jnp.sum(wi_ref[0] * d, axis=0, keepdims=True)


def dsa_sample_scores(page_table, q_idx, w_idx, cache_idx_t, li, ppc):
    N, n_pages = page_table.shape
    n_chunks = n_pages // ppc
    row = lambda n, pt: (n // n_chunks, 0, 0)
    grid_spec = pltpu.PrefetchScalarGridSpec(
        num_scalar_prefetch=1, grid=(N * n_chunks,),
        in_specs=[pl.BlockSpec((1, IDX_H, IDX_D), row), pl.BlockSpec((1, IDX_H, 1), row), pl.BlockSpec(memory_space=pl.ANY)],
        out_specs=pl.BlockSpec((1, 1, ppc * PAGE), lambda n, pt: (n // n_chunks, 0, n % n_chunks)),
        scratch_shapes=[pltpu.VMEM((2, IDX_D, ppc * PAGE), F32), pltpu.SemaphoreType.DMA((2,))])
    return pl.pallas_call(
        functools.partial(_dsa_score_kernel, n_chunks=n_chunks, ppc=ppc, li=li), grid_spec=grid_spec,
        out_shape=jax.ShapeDtypeStruct((N, 1, n_pages * PAGE), F32),
        compiler_params=_cp(("arbitrary",)), name="dsa_sample_scores")(page_table, q_idx, w_idx, cache_idx_t)


def _dsa_select_kernel(sp_ref, qi_ref, wi_ref, kin_ref, bp_ref, bn_ref, key_ref, j_ref, *, P):
    rows = sp_ref.shape[0]
    q = qi_ref[...]
    kn = kin_ref[...]
    w = wi_ref[...]
    s_new = jnp.zeros((rows, 1), F32)
    for h in range(IDX_H):
        d = jnp.sum(q[:, h * IDX_D:(h + 1) * IDX_D].astype(BF16).astype(F32) * kn.astype(BF16).astype(F32),
                    axis=-1, keepdims=True)
        s_new = s_new + w[:, h:h + 1] * jnp.maximum(d, 0.0)
    lane = lax.broadcasted_iota(I32, (rows, 128), 1)
    key_ref[:, :P] = _sortable_key(sp_ref[...])
    key_ref[:, P:] = jnp.where(lane == 0, _sortable_key(jnp.broadcast_to(s_new, (rows, 128))), INT_MIN)
    col = lax.broadcasted_iota(I32, (rows, P + 128), 1)
    bias = _topk_bias(key_ref, col <= P, DSA_TOPK, j_ref, int(math.log2(P)) + 1)
    bp_ref[...] = bias[:, :P]
    bn_ref[...] = bias[:, P:]


def dsa_sample_select(scores_past, q_idx, w_idx, k_idx_new, tr):
    N, P = scores_past.shape
    row = lambda i: (i, 0)
    return pl.pallas_call(
        functools.partial(_dsa_select_kernel, P=P), grid=(N // tr,),
        in_specs=[pl.BlockSpec((tr, P), row), pl.BlockSpec((tr, IDX_H * IDX_D), row), pl.BlockSpec((tr, 128), row),
                  pl.BlockSpec((tr, IDX_D), row)],
        out_specs=[pl.BlockSpec((tr, P), row), pl.BlockSpec((tr, 128), row)],
        out_shape=[jax.ShapeDtypeStruct((N, P), F32), jax.ShapeDtypeStruct((N, 128), F32)],
        scratch_shapes=[pltpu.VMEM((tr, P + 128), I32), pltpu.VMEM((tr, 1), I32)],
        compiler_params=_cp(("parallel",)), name="dsa_sample_select")(scores_past, q_idx, w_idx, k_idx_new)


def _dot_f32_by_01(x, m, x_is_lhs):
    a = x.astype(BF16)
    r = x - a.astype(F32)
    b = r.astype(BF16)
    c = (r - b.astype(F32)).astype(BF16)
    d = (lambda p: jnp.dot(p, m, preferred_element_type=F32)) if x_is_lhs else (
        lambda p: jnp.dot(m, p, preferred_element_type=F32))
    return d(a) + d(b) + d(c)


def _fox_suffix_kernel(pt_ref, clf, o_ref, buf, sem, *, n_pages, li):
    copies = [(lambda pg: clf.at[li, pg], lambda s, p: buf.at[s, :, p])]
    slot = _pipeline_pages(pt_ref, 1, n_pages, copies, sem)
    i0 = lax.broadcasted_iota(I32, (PAGE, PAGE), 0)
    i1 = lax.broadcasted_iota(I32, (PAGE, PAGE), 1)
    later_pos = jnp.where(i0 > i1, 1.0, 0.0).astype(BF16)
    p0 = lax.broadcasted_iota(I32, (n_pages, n_pages), 0)
    p1 = lax.broadcasted_iota(I32, (n_pages, n_pages), 1)
    later_page = jnp.where(p1 > p0, 1.0, 0.0).astype(BF16)
    for h in range(FOX_H):
        lf = buf[slot, h]
        ins = _dot_f32_by_01(lf, later_pos, True)
        tot = jnp.broadcast_to(ins[:, 0:1] + lf[:, 0:1], (n_pages, PAGE))
        o_ref[0, h] = ins + _dot_f32_by_01(tot, later_page, False)


def fox_suffix(page_table, cache_logf_t, li):
    N, n_pages = page_table.shape
    grid_spec = pltpu.PrefetchScalarGridSpec(
        num_scalar_prefetch=1, grid=(N,),
        in_specs=[pl.BlockSpec(memory_space=pl.ANY)],
        out_specs=pl.BlockSpec((1, FOX_H, n_pages, PAGE), lambda n, pt: (n, 0, 0, 0)),
        scratch_shapes=[pltpu.VMEM((2, FOX_H, n_pages, PAGE), F32), pltpu.SemaphoreType.DMA((2,))])
    return pl.pallas_call(
        functools.partial(_fox_suffix_kernel, n_pages=n_pages, li=li), grid_spec=grid_spec,
        out_shape=jax.ShapeDtypeStruct((N, FOX_H, n_pages, PAGE), F32),
        compiler_params=_cp(("arbitrary",)), name="fox_suffix")(page_table, cache_logf_t)


def _per_head_kernel(a_ref, w_ref, o_ref, *, transpose_w):
    a = a_ref[...].astype(BF16)
    w = w_ref[...].astype(BF16)
    o_ref[...] = _nt(a, w) if transpose_w else jnp.dot(a, w, preferred_element_type=F32)


def per_head_matmul(a, w, d_in, d_out, transpose_w):
    N = a.shape[0]
    H = a.shape[1] // d_in
    wb = (d_out, d_in) if transpose_w else (d_in, d_out)
    return pl.pallas_call(
        functools.partial(_per_head_kernel, transpose_w=transpose_w), grid=(H,),
        in_specs=[pl.BlockSpec((N, d_in), lambda h: (0, h)), pl.BlockSpec(wb, lambda h: (0, h))],
        out_specs=pl.BlockSpec((N, d_out), lambda h: (0, h)),
        out_shape=jax.ShapeDtypeStruct((N, H * d_out), F32),
        compiler_params=_cp(("parallel",)), name="per_head_matmul")(a, w)


def _rope_tables(pos):
    half = MLA_ROPE // 2
    inv_freq = ROPE_BASE ** (-jnp.arange(half, dtype=F32) / half)
    ang = pos.astype(F32)[:, None] * inv_freq
    cos, sin = jnp.cos(ang), jnp.sin(ang)
    z = jnp.zeros((pos.shape[0], 128 - MLA_ROPE), F32)
    return jnp.concatenate([cos, cos, z], axis=1), jnp.concatenate([-sin, sin, z], axis=1)


def _pad_cols(a, n):
    return jnp.pad(a, ((0, 0), (0, n - a.shape[1])))


def _even_weights(w_in, g_q_mla, g_krope, w_uk, w_uv):
    D = w_in.shape[0]
    o = 0
    zq = w_in[:, o:o + MLA_H * MLA_QK].reshape(D, MLA_H, MLA_QK); o += MLA_H * MLA_QK
    zq = jnp.pad(zq, ((0, 0), (0, 0), (0, MLA_PAD - MLA_QK))).reshape(D, MLA_H * MLA_PAD)
    lat = w_in[:, o:o + MLA_LAT]; o += MLA_LAT
    kr = w_in[:, o:o + MLA_ROPE]; o += MLA_ROPE
    qd = w_in[:, o:o + DSA_H * DSA_D]; o += DSA_H * DSA_D
    kd = w_in[:, o:o + DSA_G * DSA_D]; o += DSA_G * DSA_D
    vd = w_in[:, o:o + DSA_G * DSA_D]; o += DSA_G * DSA_D
    qi = w_in[:, o:o + IDX_H * IDX_D]; o += IDX_H * IDX_D
    ki = w_in[:, o:o + IDX_D]; o += IDX_D
    wi = w_in[:, o:o + IDX_H]
    w = jnp.concatenate([zq, lat, qd, kd, vd, qi, kr, ki, _pad_cols(wi, 128)], axis=1).astype(BF16)
    gq = _pad_cols(g_q_mla[None, :], MLA_PAD)
    gkr = _pad_cols(g_krope[None, :], 128)
    wk = jnp.zeros((MLA_LAT + 128, MLA_H, MLA_PAD), F32)
    wk = wk.at[:MLA_LAT, :, :MLA_NOPE].set(w_uk)
    wk = wk.at[MLA_LAT:MLA_LAT + MLA_ROPE, :, MLA_NOPE:MLA_QK].set(
        jnp.broadcast_to(jnp.eye(MLA_ROPE, dtype=F32)[:, None, :], (MLA_ROPE, MLA_H, MLA_ROPE)))
    wv = jnp.zeros((MLA_LAT + 128, MLA_H * MLA_V), F32).at[:MLA_LAT].set(w_uv.reshape(MLA_LAT, MLA_H * MLA_V))
    wkv = jnp.concatenate([wk.reshape(MLA_LAT + 128, MLA_H * MLA_PAD), wv], axis=1).astype(BF16)
    return w, gq, gkr, wkv


def _odd_weights(w_in, fox_b_f):
    w = _pad_cols(w_in, O_N).astype(BF16)
    return w, _pad_cols(fox_b_f[None, :], 128)


def kernel(x_prompt, x_sample, cache_mla_latent, cache_mla_krope, cache_dsa_k, cache_dsa_v, cache_dsa_idx_k, cache_fox_k, cache_fox_v, cache_fox_logf, state_s5_re, state_s5_im, page_table, ln_mix_even, w_in_even, g_q_mla, g_latent, g_krope, w_uk, w_uv, g_q_dsa, g_k_dsa, w_out_even, ln_mix_odd, w_in_odd, s5_lam_re, s5_lam_im, s5_log_dt, s5_b_re, s5_b_im, s5_c_re, s5_c_im, s5_d, s5_w_glu, s5_b_glu, g_q_fox, g_k_fox, fox_b_f, w_out_odd, ln_mlp, w_up, w_down):
    B, S, D = x_prompt.shape
    N = x_sample.shape[0]
    n_pages = page_table.shape[1]
    P = n_pages * PAGE
    MP = B * S
    TM = 512
    TA = 512
    TS = 512
    PPC = 64
    yp = x_prompt.reshape(MP, D)
    ys = x_sample.reshape(N, D)
    cos_p, sin_p = _rope_tables(jnp.arange(S))
    cos_s, sin_s = _rope_tables(jnp.full((N,), P, I32))
    outs = {}
    depth = ln_mlp.shape[0]
    for layer in range(depth):
        li = layer // 2
        if layer % 2 == 0:
            w, gq, gkr, wkv = _even_weights(w_in_even[li], g_q_mla[li], g_krope[li], w_uk[li], w_uv[li])
            gains = (gq, g_latent[li][None], gkr, g_q_dsa[li][None], g_k_dsa[li][None])
            w_out = w_out_even[li].astype(BF16)
            w_o1, w_o2 = w_out[:MLA_H * MLA_V], w_out[MLA_H * MLA_V:]
            z = rms_matmul(yp, ln_mix_even[li][None], w, TM, E_N // 2)
            qm, lat, kr, lk, qd, kd, vd, qi, ki, wi = even_post(z, cos_p, sin_p, *gains, 256, BF16, S // 256)
            kv = matmul(lk, wkv, BF16, TM, wkv.shape[1] // 3)
            r3 = lambda a: a.reshape(B, S, a.shape[1])
            RM = MLA_H // 2
            o_mla = flash_attention(r3(qm), r3(kv), r3(kv), G=2, R=RM, dq=MLA_PAD, dv=MLA_V, T=TA, TS=TS,
                                    shared_kv=False, v_block0=MLA_H * MLA_PAD // (RM * MLA_V))
            wi_t = r3(wi)[:, :, :8].transpose(0, 2, 1)
            mask = dsa_prompt_mask_t(r3(qi), wi_t, r3(ki), 128, TA)
            o_dsa = flash_attention(r3(qd), r3(kd), r3(vd), G=DSA_G, R=DSA_H // DSA_G, dq=DSA_D, dv=DSA_D, T=TA, TS=TS,
                                    mask=mask)
            yp = out_proj(yp, o_mla.reshape(MP, -1), o_dsa.reshape(MP, -1), w_o1, w_o2, TM, D)
            outs.setdefault('lat_p', []).append(lat.reshape(B, S, MLA_LAT))
            outs.setdefault('kr_p', []).append(kr.reshape(B, S, MLA_ROPE))
            outs.setdefault('dk_p', []).append(kd.reshape(B, S, DSA_G, DSA_D))
            outs.setdefault('dv_p', []).append(vd.reshape(B, S, DSA_G, DSA_D))
            outs.setdefault('di_p', []).append(ki.reshape(B, S, IDX_D))
            z = rms_matmul(ys, ln_mix_even[li][None], w, N, E_N // 2)
            qm, lat, kr, lk, qd, kd, vd, qi, ki, wi = even_post(z, cos_s, sin_s, *gains, N, F32, 1)
            qm3 = qm.reshape(N, MLA_H, MLA_PAD)
            q_nope = qm3[:, :, :MLA_NOPE].reshape(N, MLA_H * MLA_NOPE)
            q_rope = qm3[:, :, MLA_NOPE:MLA_QK]
            w_uk2 = w_uk[li].reshape(MLA_LAT, MLA_H * MLA_NOPE)
            q_lat = per_head_matmul(q_nope, w_uk2, MLA_NOPE, MLA_LAT, True).reshape(N, MLA_H, MLA_LAT)
            o_lat = mla_decode(page_table, q_lat, q_rope, lat[:, None, :], kr[:, None, :],
                               cache_mla_latent, jnp.swapaxes(cache_mla_krope, 2, 3), li, PPC)
            o_mla = per_head_matmul(o_lat.reshape(N, MLA_H * MLA_LAT), w_uv[li].reshape(MLA_LAT, MLA_H * MLA_V),
                                    MLA_LAT, MLA_V, False)
            sc = dsa_sample_scores(page_table, qi.reshape(N, IDX_H, IDX_D), wi[:, :IDX_H, None],
                                   jnp.swapaxes(cache_dsa_idx_k, 2, 3), li, PPC)
            bias_p, bias_n = dsa_sample_select(sc.reshape(N, P), qi, wi, ki, 32)
            o_dsa = gqa_decode(page_table, qd.reshape(N, DSA_H, DSA_D), kd.reshape(N, DSA_G, DSA_D),
                               vd.reshape(N, DSA_G, DSA_D), bias_p[:, None, :], bias_n[:, None, :],
                               cache_dsa_k, cache_dsa_v, li, PPC, 'mask')
            ys = out_proj(ys, o_mla, o_dsa.reshape(N, -1), w_o1, w_o2, N, 1024)
            outs.setdefault('lat_s', []).append(lat.reshape(N, 1, MLA_LAT))
            outs.setdefault('kr_s', []).append(kr.reshape(N, 1, MLA_ROPE))
            outs.setdefault('dk_s', []).append(kd.reshape(N, 1, DSA_G, DSA_D))
            outs.setdefault('dv_s', []).append(vd.reshape(N, 1, DSA_G, DSA_D))
            outs.setdefault('di_s', []).append(ki.reshape(N, 1, IDX_D))
        else:
            w, bf = _odd_weights(w_in_odd[li], fox_b_f[li])
            s5p = (s5_lam_re[li], s5_lam_im[li], s5_log_dt[li], s5_b_re[li], s5_b_im[li], s5_c_re[li], s5_c_im[li])
            w_glu = s5_w_glu[li].astype(BF16)
            w_out = w_out_odd[li].astype(BF16)
            w_o1, w_o2 = w_out[:S5_CH], w_out[S5_CH:]
            R = FOX_H // FOX_G
            z = rms_matmul(yp, ln_mix_odd[li][None], w, TM, O_N // 3)
            u, q, k, v, lf, cum = odd_post(z, g_q_fox[li][None], g_k_fox[li][None], bf, 256, BF16, B, True)
            y5, sre, sim = s5_prompt(u.reshape(B, S, S5_CH), *s5p)
            o_s5 = s5_glu(y5, u, s5_d[li][None], w_glu, s5_b_glu[li][None], TM)
            cum3 = cum[:, :FOX_H].reshape(B, S, FOX_G, R)
            cq = jnp.pad(cum3.transpose(0, 2, 1, 3), ((0, 0), (0, 0), (0, 0), (0, 128 - R)))
            ck = jnp.pad(cum3.transpose(0, 2, 3, 1), ((0, 0), (0, 0), (0, 8 - R), (0, 0)))
            r3 = lambda a: a.reshape(B, S, a.shape[1])
            o_fox = flash_attention(r3(q), r3(k), r3(v), G=FOX_G, R=R, dq=FOX_D, dv=FOX_D, T=TA, TS=TS, cq=cq, ck=ck)
            yp = out_proj(yp, o_s5, o_fox.reshape(MP, -1), w_o1, w_o2, TM, D)
            outs.setdefault('fk_p', []).append(k.reshape(B, S, FOX_G, FOX_D))
            outs.setdefault('fv_p', []).append(v.reshape(B, S, FOX_G, FOX_D))
            outs.setdefault('fl_p', []).append(lf[:, :FOX_H].reshape(B, S, FOX_H))
            outs.setdefault('sre_p', []).append(sre)
            outs.setdefault('sim_p', []).append(sim)
            z = rms_matmul(ys, ln_mix_odd[li][None], w, N, O_N // 3)
            u, q, k, v, lf, _ = odd_post(z, g_q_fox[li][None], g_k_fox[li][None], bf, N, F32, 1, False)
            y5, sre, sim = s5_sample(u, state_s5_re[li], state_s5_im[li], *s5p)
            o_s5 = s5_glu(y5, u, s5_d[li][None], w_glu, s5_b_glu[li][None], N)
            sfx = fox_suffix(page_table, jnp.swapaxes(cache_fox_logf, 2, 3), li)
            o_fox = gqa_decode(page_table, q.reshape(N, FOX_H, FOX_D), k.reshape(N, FOX_G, FOX_D),
                               v.reshape(N, FOX_G, FOX_D), sfx, lf[:, :FOX_H, None],
                               cache_fox_k, cache_fox_v, li, PPC, 'forget')
            ys = out_proj(ys, o_s5, o_fox.reshape(N, -1), w_o1, w_o2, N, 1024)
            outs.setdefault('fk_s', []).append(k.reshape(N, 1, FOX_G, FOX_D))
            outs.setdefault('fv_s', []).append(v.reshape(N, 1, FOX_G, FOX_D))
            outs.setdefault('fl_s', []).append(lf[:, :FOX_H].reshape(N, 1, FOX_H))
            outs.setdefault('sre_s', []).append(sre)
            outs.setdefault('sim_s', []).append(sim)
        wu = w_up[layer].astype(BF16)
        wd = w_down[layer].astype(BF16)
        yp = mlp(yp, ln_mlp[layer][None], wu, wd, TM, 512)
        ys = mlp(ys, ln_mlp[layer][None], wu, wd, N, 512)
    order = ('lat_p', 'lat_s', 'kr_p', 'kr_s', 'dk_p', 'dk_s', 'dv_p', 'dv_s', 'di_p', 'di_s',
             'fk_p', 'fk_s', 'fv_p', 'fv_s', 'fl_p', 'fl_s', 'sre_p', 'sre_s', 'sim_p', 'sim_s')
    return (yp.reshape(B, S, D), ys.reshape(N, 1, D)) + tuple(jnp.stack(outs[name]) for name in order)
```

```python
import functools
import math

import numpy as np
import jax
import jax.numpy as jnp
from jax import lax
from jax.experimental import pallas as pl
from jax.experimental.pallas import tpu as pltpu

F32 = jnp.float32
BF16 = jnp.bfloat16
I32 = jnp.int32

LANES = 128
VMEM_LIMIT_BYTES = 56 * 1024 * 1024

D_MODEL = 2048
PAGE = 128
EPS = 1e-6
ROPE_BASE = 10000.0
MLA_H, MLA_NOPE, MLA_ROPE, MLA_V, MLA_LAT = 8, 128, 64, 128, 256
MLA_QK = MLA_NOPE + MLA_ROPE
MLA_PAD = 256
DSA_H, DSA_G, DSA_D = 8, 2, 128
IDX_H, IDX_D = 4, 64
DSA_TOPK = 256
S5_CH, S5_GRP, S5_G, S5_P = 1024, 16, 64, 64
S5_L = 16
FOX_H, FOX_G, FOX_D = 8, 2, 128
D_FF = 4 * D_MODEL
NEG = -1e30
INT_MIN = -(2 ** 31)
LOG2E = math.log2(math.e)

E_Q, E_LAT, E_QD, E_KD, E_VD, E_QI, E_KK, E_WI, E_N = 0, 2048, 2304, 3328, 3584, 3840, 4096, 4224, 4352
O_U, O_Q, O_K, O_V, O_F, O_N = 0, 1024, 2048, 2304, 2560, 2688


def _cp(sem):
    return pltpu.CompilerParams(dimension_semantics=sem, vmem_limit_bytes=VMEM_LIMIT_BYTES)


def _nt(a, b, precision=None):
    return lax.dot_general(a, b, (((1,), (1,)), ((), ())), preferred_element_type=F32, precision=precision)


def _rms_mm_kernel(x_ref, g_ref, w_ref, o_ref, h_ref):
    @pl.when(pl.program_id(1) == 0)
    def _():
        x = x_ref[...]
        ms = jnp.mean(x * x, axis=-1, keepdims=True)
        h_ref[...] = ((x * lax.rsqrt(ms + EPS)) * g_ref[...]).astype(BF16)

    o_ref[...] = jnp.dot(h_ref[...], w_ref[...], preferred_element_type=F32)


def rms_matmul(x, g, w, tm, tn):
    M, K = x.shape
    N = w.shape[1]
    return pl.pallas_call(
        _rms_mm_kernel, grid=(M // tm, N // tn),
        in_specs=[pl.BlockSpec((tm, K), lambda i, j: (i, 0)),
                  pl.BlockSpec((1, K), lambda i, j: (0, 0)),
                  pl.BlockSpec((K, tn), lambda i, j: (0, j))],
        out_specs=pl.BlockSpec((tm, tn), lambda i, j: (i, j)),
        out_shape=jax.ShapeDtypeStruct((M, N), F32),
        scratch_shapes=[pltpu.VMEM((tm, K), BF16)],
        compiler_params=_cp(("parallel", "arbitrary")), name="rms_matmul")(x, g, w)


def _mm_kernel(a_ref, w_ref, o_ref):
    o_ref[...] = jnp.dot(a_ref[...].astype(BF16), w_ref[...], preferred_element_type=F32).astype(o_ref.dtype)


def matmul(a, w, out_dtype, tm, tn):
    M, K = a.shape
    N = w.shape[1]
    return pl.pallas_call(
        _mm_kernel, grid=(M // tm, N // tn),
        in_specs=[pl.BlockSpec((tm, K), lambda i, j: (i, 0)), pl.BlockSpec((K, tn), lambda i, j: (0, j))],
        out_specs=pl.BlockSpec((tm, tn), lambda i, j: (i, j)),
        out_shape=jax.ShapeDtypeStruct((M, N), out_dtype),
        compiler_params=_cp(("parallel", "arbitrary")), name="matmul")(a, w)


def _out_proj_kernel(x_ref, a1_ref, a2_ref, w1_ref, w2_ref, o_ref):
    acc = jnp.dot(a1_ref[...].astype(BF16), w1_ref[...], preferred_element_type=F32)
    acc = acc + jnp.dot(a2_ref[...].astype(BF16), w2_ref[...], preferred_element_type=F32)
    o_ref[...] = x_ref[...] + acc


def out_proj(x, a1, a2, w, tm, tn):
    M, N = x.shape
    K1, K2 = a1.shape[1], a2.shape[1]
    assert K1 == K2 and w.shape[0] == K1 + K2
    return pl.pallas_call(
        _out_proj_kernel, grid=(M // tm, N // tn),
        in_specs=[pl.BlockSpec((tm, tn), lambda i, j: (i, j)),
                  pl.BlockSpec((tm, K1), lambda i, j: (i, 0)),
                  pl.BlockSpec((tm, K2), lambda i, j: (i, 0)),
                  pl.BlockSpec((K1, tn), lambda i, j: (0, j)),
                  pl.BlockSpec((K2, tn), lambda i, j: (1, j))],
        out_specs=pl.BlockSpec((tm, tn), lambda i, j: (i, j)),
        out_shape=jax.ShapeDtypeStruct((M, N), F32),
        compiler_params=_cp(("parallel", "arbitrary")), name="out_proj")(x, a1, a2, w, w)


def _mlp_kernel(x_ref, g_ref, wu_ref, wd_ref, o_ref, h_ref, acc_ref):
    f = pl.program_id(1)

    @pl.when(f == 0)
    def _():
        x = x_ref[...]
        ms = jnp.mean(x * x, axis=-1, keepdims=True)
        h_ref[...] = ((x * lax.rsqrt(ms + EPS)) * g_ref[...]).astype(BF16)
        acc_ref[...] = jnp.zeros_like(acc_ref)

    up = jnp.dot(h_ref[...], wu_ref[...], preferred_element_type=F32)
    act = jnp.square(jnp.maximum(up, 0.0)).astype(BF16)
    acc_ref[...] += jnp.dot(act, wd_ref[...], preferred_element_type=F32)

    @pl.when(f == pl.num_programs(1) - 1)
    def _():
        o_ref[...] = x_ref[...] + acc_ref[...]


def mlp(x, g, w_up, w_down, tm, tf):
    M, D = x.shape
    FF = w_up.shape[1]
    return pl.pallas_call(
        _mlp_kernel, grid=(M // tm, FF // tf),
        in_specs=[pl.BlockSpec((tm, D), lambda i, f: (i, 0)),
                  pl.BlockSpec((1, D), lambda i, f: (0, 0)),
                  pl.BlockSpec((D, tf), lambda i, f: (0, f)),
                  pl.BlockSpec((tf, D), lambda i, f: (f, 0))],
        out_specs=pl.BlockSpec((tm, D), lambda i, f: (i, 0)),
        out_shape=jax.ShapeDtypeStruct((M, D), F32),
        scratch_shapes=[pltpu.VMEM((tm, D), BF16), pltpu.VMEM((tm, D), F32)],
        compiler_params=_cp(("parallel", "arbitrary")), name="mlp")(x, g, w_up, w_down)


def _rope128(r, c, s):
    lane = lax.broadcasted_iota(I32, r.shape, 1)
    partner = jnp.where(lane < 32, pltpu.roll(r, 96, 1), pltpu.roll(r, 32, 1))
    return r * c + partner * s


def _even_post_kernel(z_ref, cos_ref, sin_ref, gq_ref, gl_ref, gkr_ref, gqd_ref, gkd_ref,
                      qm_ref, lat_ref, kr_ref, lk_ref, qd_ref, kd_ref, vd_ref, qi_ref, ki_ref, wi_ref):
    c = cos_ref[...]
    s = sin_ref[...]
    mla_scale = MLA_QK ** -0.5 * LOG2E
    dsa_scale = DSA_D ** -0.5 * LOG2E
    gq = gq_ref[...]
    for h in range(MLA_H):
        x = z_ref[:, E_Q + h * MLA_PAD:E_Q + (h + 1) * MLA_PAD]
        ms = jnp.sum(x * x, axis=-1, keepdims=True) * (1.0 / MLA_QK)
        y = (x * lax.rsqrt(ms + EPS)) * gq
        qm_ref[:, h * MLA_PAD:h * MLA_PAD + 128] = (y[:, :128] * mla_scale).astype(qm_ref.dtype)
        qm_ref[:, h * MLA_PAD + 128:(h + 1) * MLA_PAD] = (_rope128(y[:, 128:], c, s) * mla_scale).astype(qm_ref.dtype)
    x = z_ref[:, E_LAT:E_LAT + MLA_LAT]
    ms = jnp.mean(x * x, axis=-1, keepdims=True)
    lat = (x * lax.rsqrt(ms + EPS)) * gl_ref[...]
    lat_ref[...] = lat
    zk = z_ref[:, E_KK:E_KK + 128]
    lane = lax.broadcasted_iota(I32, zk.shape, 1)
    kr_in = jnp.where(lane < MLA_ROPE, zk, 0.0)
    ms = jnp.sum(kr_in * kr_in, axis=-1, keepdims=True) * (1.0 / MLA_ROPE)
    kr = _rope128((kr_in * lax.rsqrt(ms + EPS)) * gkr_ref[...], c, s)
    kr_ref[...] = kr[:, :MLA_ROPE]
    ki_ref[...] = zk[:, MLA_ROPE:]
    lk_ref[:, :MLA_LAT] = lat.astype(BF16)
    lk_ref[:, MLA_LAT:] = kr.astype(BF16)
    for h in range(DSA_H):
        x = z_ref[:, E_QD + h * DSA_D:E_QD + (h + 1) * DSA_D]
        ms = jnp.mean(x * x, axis=-1, keepdims=True)
        qd_ref[:, h * DSA_D:(h + 1) * DSA_D] = ((x * lax.rsqrt(ms + EPS)) * gqd_ref[...] * dsa_scale).astype(qd_ref.dtype)
    for h in range(DSA_G):
        x = z_ref[:, E_KD + h * DSA_D:E_KD + (h + 1) * DSA_D]
        ms = jnp.mean(x * x, axis=-1, keepdims=True)
        kd_ref[:, h * DSA_D:(h + 1) * DSA_D] = (x * lax.rsqrt(ms + EPS)) * gkd_ref[...]
    vd_ref[...] = z_ref[:, E_VD:E_VD + DSA_G * DSA_D]
    qi_ref[...] = z_ref[:, E_QI:E_QI + IDX_H * IDX_D].astype(qi_ref.dtype)
    wi_ref[...] = z_ref[:, E_WI:E_WI + 128]


def even_post(z, cos, sin, gq, gl, gkr, gqd, gkd, tm, q_dtype, n_pos_blocks):
    M = z.shape[0]
    row = lambda i: (i, 0)
    pos = lambda i: (i % n_pos_blocks, 0)
    fix = lambda i: (0, 0)
    widths = [(MLA_H * MLA_PAD, q_dtype), (MLA_LAT, F32), (MLA_ROPE, F32), (MLA_LAT + 128, BF16),
              (DSA_H * DSA_D, q_dtype), (DSA_G * DSA_D, F32), (DSA_G * DSA_D, F32),
              (IDX_H * IDX_D, q_dtype), (IDX_D, F32), (128, F32)]
    return pl.pallas_call(
        _even_post_kernel, grid=(M // tm,),
        in_specs=[pl.BlockSpec((tm, E_N), row), pl.BlockSpec((tm, 128), pos), pl.BlockSpec((tm, 128), pos),
                  pl.BlockSpec((1, MLA_PAD), fix), pl.BlockSpec((1, MLA_LAT), fix), pl.BlockSpec((1, 128), fix),
                  pl.BlockSpec((1, DSA_D), fix), pl.BlockSpec((1, DSA_D), fix)],
        out_specs=[pl.BlockSpec((tm, w), row) for w, _ in widths],
        out_shape=[jax.ShapeDtypeStruct((M, w), dt) for w, dt in widths],
        compiler_params=_cp(("parallel",)), name="even_post")(z, cos, sin, gq, gl, gkr, gqd, gkd)


def _log_sigmoid(x):
    return -(jnp.maximum(-x, 0.0) + jnp.log1p(jnp.exp(-jnp.abs(x))))


def _odd_post_kernel(z_ref, gq_ref, gk_ref, bf_ref, u_ref, q_ref, k_ref, v_ref, lf_ref, cum_ref, carry_ref, *, cumulative):
    scale = FOX_D ** -0.5 * LOG2E
    u_ref[...] = z_ref[:, O_U:O_U + S5_CH]
    for h in range(FOX_H):
        x = z_ref[:, O_Q + h * FOX_D:O_Q + (h + 1) * FOX_D]
        ms = jnp.mean(x * x, axis=-1, keepdims=True)
        q_ref[:, h * FOX_D:(h + 1) * FOX_D] = ((x * lax.rsqrt(ms + EPS)) * gq_ref[...] * scale).astype(q_ref.dtype)
    for h in range(FOX_G):
        x = z_ref[:, O_K + h * FOX_D:O_K + (h + 1) * FOX_D]
        ms = jnp.mean(x * x, axis=-1, keepdims=True)
        k_ref[:, h * FOX_D:(h + 1) * FOX_D] = (x * lax.rsqrt(ms + EPS)) * gk_ref[...]
    v_ref[...] = z_ref[:, O_V:O_V + FOX_G * FOX_D]
    zf = z_ref[:, O_F:O_F + 128]
    lane = lax.broadcasted_iota(I32, zf.shape, 1)
    lf = jnp.where(lane < FOX_H, _log_sigmoid(zf + bf_ref[...]), 0.0)
    lf_ref[...] = lf
    if cumulative:
        tm = zf.shape[0]

        @pl.when(pl.program_id(1) == 0)
        def _():
            carry_ref[...] = jnp.zeros_like(carry_ref)

        r = lax.broadcasted_iota(I32, (tm, tm), 0)
        cidx = lax.broadcasted_iota(I32, (tm, tm), 1)
        tri = jnp.where(cidx <= r, 1.0, 0.0).astype(F32)
        cum = jnp.dot(tri, lf, preferred_element_type=F32, precision=lax.Precision.HIGHEST) + carry_ref[...]
        cum_ref[...] = cum * LOG2E
        carry_ref[...] = cum[tm - 1:tm, :]
    else:
        cum_ref[...] = lf


def odd_post(z, gq, gk, bf, tm, q_dtype, n_batch, cumulative):
    M = z.shape[0]
    nb = M // tm // n_batch
    row = lambda b, i: (b * nb + i, 0)
    fix = lambda b, i: (0, 0)
    widths = [(S5_CH, F32), (FOX_H * FOX_D, q_dtype), (FOX_G * FOX_D, F32), (FOX_G * FOX_D, F32), (128, F32), (128, F32)]
    return pl.pallas_call(
        functools.partial(_odd_post_kernel, cumulative=cumulative), grid=(n_batch, nb),
        in_specs=[pl.BlockSpec((tm, O_N), row), pl.BlockSpec((1, FOX_D), fix), pl.BlockSpec((1, FOX_D), fix),
                  pl.BlockSpec((1, 128), fix)],
        out_specs=[pl.BlockSpec((tm, w), row) for w, _ in widths],
        out_shape=[jax.ShapeDtypeStruct((M, w), dt) for w, dt in widths],
        scratch_shapes=[pltpu.VMEM((1, 128), F32)],
        compiler_params=_cp(("arbitrary", "arbitrary")), name="odd_post")(z, gq, gk, bf)


def _flash_kernel(qi_ref, ki_ref, *refs, R, dq, dv, T, TS, shared_kv, has_bias, has_mask):
    q_ref, k_ref, v_ref = refs[:3]
    pos = 3
    cq_ref = ck_ref = mask_ref = None
    if has_bias:
        cq_ref, ck_ref = refs[pos:pos + 2]
        pos += 2
    if has_mask:
        mask_ref = refs[pos]
        pos += 1
    o_ref, m_ref, l_ref, acc_ref = refs[pos:pos + 4]
    t = pl.program_id(2)
    qi = qi_ref[t]
    ki = ki_ref[t]

    @pl.when(ki == 0)
    def _():
        m_ref[...] = jnp.full_like(m_ref, NEG)
        l_ref[...] = jnp.zeros_like(l_ref)
        acc_ref[...] = jnp.zeros_like(acc_ref)

    def step(diag):
        for r in range(R):
            kr = 0 if shared_kv else r
            if r == 0 or not shared_kv:
                k = k_ref[0, :, kr * dq:(kr + 1) * dq].astype(BF16)
                v = v_ref[0, :, kr * dv:(kr + 1) * dv].astype(BF16)
            for qs in range(T // TS):
                rows = slice(qs * TS, (qs + 1) * TS)
                ncol = (qs + 1) * TS if diag else T
                s = _nt(q_ref[0, rows, r * dq:(r + 1) * dq], k[:ncol])
                if has_bias:
                    s = s + (cq_ref[0, 0, rows, r:r + 1] - ck_ref[0, 0, r:r + 1, :ncol])
                if has_mask:
                    mask = mask_ref[0, :, 0].reshape(T, T)
                    s = s + mask[rows, :ncol].astype(F32)
                elif diag:
                    row = qs * TS + lax.broadcasted_iota(I32, (TS, ncol), 0)
                    col = lax.broadcasted_iota(I32, (TS, ncol), 1)
                    s = jnp.where(col <= row, s, NEG)
                m_prev = m_ref[r, rows]
                m_new = jnp.maximum(m_prev, jnp.max(s, axis=1, keepdims=True))
                alpha = jnp.exp2(m_prev - m_new)
                ps = [jnp.exp2(s[:, c * LANES:(c + 1) * LANES] - m_new) for c in range(ncol // LANES)]
                l_ref[r, rows] = alpha * l_ref[r, rows] + functools.reduce(lambda a, b: a + b, ps)
                p = jnp.concatenate(ps, axis=1).astype(BF16)
                acc_ref[r, rows] = alpha * acc_ref[r, rows] + jnp.dot(p, v[:ncol], preferred_element_type=F32)
                m_ref[r, rows] = m_new

    @pl.when(ki < qi)
    def _():
        step(False)

    @pl.when(ki == qi)
    def _():
        step(True)
        for r in range(R):
            l_tot = jnp.sum(l_ref[r], axis=1, keepdims=True)
            o_ref[0, :, r * dv:(r + 1) * dv] = (acc_ref[r] / l_tot).astype(o_ref.dtype)


def flash_attention(q, k, v, *, G, R, dq, dv, T, TS, shared_kv=True, cq=None, ck=None, mask=None, k_block0=0, v_block0=0):
    assert dv == LANES
    B, S, _ = q.shape
    n = S // T
    pairs = [(i, j) for i in range(n) for j in range(i + 1)]
    qi_list = jnp.asarray([p[0] for p in pairs], I32)
    ki_list = jnp.asarray([p[1] for p in pairs], I32)
    kw = 1 if shared_kv else R
    qmap = lambda b, g, t, qi, ki: (b, qi[t], g)
    kmap = lambda b, g, t, qi, ki: (b, ki[t], k_block0 + g)
    vmap = lambda b, g, t, qi, ki: (b, ki[t], v_block0 + g)
    in_specs = [pl.BlockSpec((1, T, R * dq), qmap), pl.BlockSpec((1, T, kw * dq), kmap), pl.BlockSpec((1, T, kw * dv), vmap)]
    args = [q, k, v]
    if cq is not None:
        in_specs += [pl.BlockSpec((1, 1, T, 128), lambda b, g, t, qi, ki: (b, g, qi[t], 0)),
                     pl.BlockSpec((1, 1, 8, T), lambda b, g, t, qi, ki: (b, g, 0, ki[t]))]
        args += [cq, ck]
    if mask is not None:
        tq = mask.shape[3]
        assert mask.shape[4] == T and T % tq == 0
        in_specs += [pl.BlockSpec((1, T // tq, 1, tq, T), lambda b, g, t, qi, ki: (b, qi[t], ki[t], 0, 0))]
        args += [mask]
    kern = functools.partial(_flash_kernel, R=R, dq=dq, dv=dv, T=T, TS=TS, shared_kv=shared_kv,
                             has_bias=cq is not None, has_mask=mask is not None)
    grid_spec = pltpu.PrefetchScalarGridSpec(
        num_scalar_prefetch=2, grid=(B, G, len(pairs)), in_specs=in_specs,
        out_specs=pl.BlockSpec((1, T, R * dv), qmap),
        scratch_shapes=[pltpu.VMEM((R, T, LANES), F32), pltpu.VMEM((R, T, LANES), F32), pltpu.VMEM((R, T, dv), F32)])
    return pl.pallas_call(
        kern, grid_spec=grid_spec, out_shape=jax.ShapeDtypeStruct((B, S, G * R * dv), BF16),
        compiler_params=_cp(("parallel", "parallel", "arbitrary")), name="flash_attention")(qi_list, ki_list, *args)


def _sortable_key(score):
    bits = lax.bitcast_convert_type(score, I32)
    key = bits ^ ((bits >> 31) & jnp.int32(0x7FFFFFFF))
    return jnp.where(key == -1, 0, key)


def _count(mask):
    return jnp.sum(jnp.where(mask, 1.0, 0.0), axis=1, keepdims=True)


def _topk_bias(key_ref, valid, k, j_ref, n_idx_bits):
    rows, cols = key_ref.shape
    kf = float(k)

    def kth(i, t):
        cand = t + jnp.left_shift(jnp.int32(1), 31 - i)
        cnt = _count(key_ref[...] >= cand)
        return jnp.where(cnt >= kf, cand, t)

    thr = lax.fori_loop(0, 32, kth, jnp.full((rows, 1), INT_MIN, I32))
    key = key_ref[...]
    gt = key > thr
    eq = (key == thr) & valid
    need = kf - _count(gt)
    n_eq = _count(eq)
    col = lax.broadcasted_iota(I32, (rows, cols), 1)
    j_ref[...] = jnp.full((rows, 1), cols, I32)
    overfull = jnp.max(jnp.where(n_eq > need, 1.0, 0.0))

    @pl.when(overfull > 0.0)
    def _():
        def cut(i, j0):
            cand = j0 + jnp.left_shift(jnp.int32(1), n_idx_bits - 1 - i)
            cnt = _count((key_ref[...] == thr) & valid & (col < cand))
            return jnp.where(cnt < need, cand, j0)

        j_ref[...] = lax.fori_loop(0, n_idx_bits, cut, jnp.zeros((rows, 1), I32))

    sel = valid & (gt | (eq & (col <= j_ref[...])))
    return jnp.where(sel, 0.0, NEG)


def _dsa_mask_t_kernel(qi_ref, wt_ref, ki_ref, bias_ref, key_ref, tie_ref, j_ref, *, tq, cw, S):
    qb = pl.program_id(1)
    n_all = S // cw
    nc = (qb * tq + tq - 1) // cw + 1
    q = qi_ref[0]
    wt = wt_ref[0]
    kf = float(DSA_TOPK)
    krow = lax.broadcasted_iota(I32, (cw, tq), 0)
    qidx = qb * tq + lax.broadcasted_iota(I32, (cw, tq), 1)

    def score_chunk(c, carry):
        kb = ki_ref[0, pl.ds(pl.multiple_of(c * cw, cw), cw), :].astype(BF16)
        sc = jnp.zeros((cw, tq), F32)
        for h in range(IDX_H):
            sc = sc + wt[h:h + 1, :] * jnp.maximum(_nt(kb, q[:, h * IDX_D:(h + 1) * IDX_D]), 0.0)
        key_ref[c] = jnp.where(c * cw + krow <= qidx, _sortable_key(sc), INT_MIN)
        return carry

    lax.fori_loop(0, nc, score_chunk, 0)

    def count(ref, pred):
        def body(c, acc):
            hit = jnp.where(pred(ref[c]), 1.0, 0.0)
            return acc + jnp.sum(hit.reshape(cw // 64, 64, tq), axis=0)
        return jnp.sum(lax.fori_loop(0, nc, body, jnp.zeros((64, tq), F32)), axis=0, keepdims=True)

    def kth(i, t):
        cand = t + jnp.left_shift(jnp.int32(1), 31 - i)
        return jnp.where(count(key_ref, lambda k: k >= cand) >= kf, cand, t)

    thr = lax.fori_loop(0, 32, kth, jnp.full((1, tq), INT_MIN, I32))
    need = kf - count(key_ref, lambda k: k > thr)

    def tie_chunk(c, carry):
        kidx = c * cw + krow
        tie_ref[c] = jnp.where((key_ref[c] == thr) & (kidx <= qidx), kidx, 2 * S)
        return carry

    lax.fori_loop(0, nc, tie_chunk, 0)
    n_eq = count(tie_ref, lambda t: t < 2 * S)
    j_ref[...] = jnp.full((1, tq), S, I32)
    overfull = jnp.max(jnp.where(n_eq > need, 1.0, 0.0))

    @pl.when(overfull > 0.0)
    def _():
        def cut(i, j0):
            cand = j0 + jnp.left_shift(jnp.int32(1), int(math.log2(S)) - 1 - i)
            return jnp.where(count(tie_ref, lambda t: t < cand) < need, cand, j0)

        j_ref[...] = lax.fori_loop(0, int(math.log2(S)), cut, jnp.zeros((1, tq), I32))

    jcut = j_ref[...]

    def write_chunk(c, carry):
        sel = (key_ref[c] > thr) | (tie_ref[c] <= jcut)
        bias_ref[0, 0, c] = jnp.where(sel, 0.0, NEG).T.astype(BF16)
        return carry

    lax.fori_loop(0, nc, write_chunk, 0)

    def fill_chunk(c, carry):
        bias_ref[0, 0, c] = jnp.full((tq, cw), NEG, BF16)
        return carry

    lax.fori_loop(nc, n_all, fill_chunk, 0)


def dsa_prompt_mask_t(q_idx, w_idx_t, k_idx, tq, cw):
    B, S, _ = q_idx.shape
    return pl.pallas_call(
        functools.partial(_dsa_mask_t_kernel, tq=tq, cw=cw, S=S), grid=(B, S // tq),
        in_specs=[pl.BlockSpec((1, tq, IDX_H * IDX_D), lambda b, i: (b, i, 0)),
                  pl.BlockSpec((1, 8, tq), lambda b, i: (b, 0, i)),
                  pl.BlockSpec((1, S, IDX_D), lambda b, i: (b, 0, 0))],
        out_specs=pl.BlockSpec((1, 1, S // cw, tq, cw), lambda b, i: (b, i, 0, 0, 0)),
        out_shape=jax.ShapeDtypeStruct((B, S // tq, S // cw, tq, cw), BF16),
        scratch_shapes=[pltpu.VMEM((S // cw, cw, tq), I32), pltpu.VMEM((S // cw, cw, tq), I32), pltpu.VMEM((1, tq), I32)],
        compiler_params=_cp(("parallel", "parallel")), name="dsa_prompt_mask")(q_idx, w_idx_t, k_idx)


def _s5_params(lam_re, lam_im, log_dt, b_re, b_im, c_re, c_im):
    dt = jnp.exp(log_dt)[:, None]
    def apow(n):
        mag = jnp.exp(lam_re * dt * n)
        return mag * jnp.cos(lam_im * dt * n), mag * jnp.sin(lam_im * dt * n)
    a_re, a_im = apow(1.0)
    den = lam_re * lam_re + lam_im * lam_im
    x, y = a_re - 1.0, a_im
    co_re = (x * lam_re + y * lam_im) / den
    co_im = (y * lam_re - x * lam_im) / den
    bb_re = co_re[..., None] * b_re - co_im[..., None] * b_im
    bb_im = co_re[..., None] * b_im + co_im[..., None] * b_re
    return a_re, a_im, bb_re, bb_im, apow


def _s5_chunk_operators(lam_re, lam_im, log_dt, b_re, b_im, c_re, c_im):
    a_re, a_im, bb_re, bb_im, apow = _s5_params(lam_re, lam_im, log_dt, b_re, b_im, c_re, c_im)
    L = S5_L
    pw = [apow(float(t)) for t in range(L + 1)]
    pw_re = jnp.stack([p[0] for p in pw])
    pw_im = jnp.stack([p[1] for p in pw])
    ab_re = pw_re[..., None] * bb_re[None] - pw_im[..., None] * bb_im[None]
    ab_im = pw_re[..., None] * bb_im[None] + pw_im[..., None] * bb_re[None]
    hp = lax.Precision.HIGHEST
    kern = (jnp.einsum('gcp,tgpd->tgcd', c_re, ab_re[:L], precision=hp)
            - jnp.einsum('gcp,tgpd->tgcd', c_im, ab_im[:L], precision=hp))
    w_re = ab_re[L - 1 - np.arange(L)]
    w_im = ab_im[L - 1 - np.arange(L)]
    w = jnp.concatenate([w_re, w_im], axis=2)
    w = w.transpose(1, 0, 3, 2).reshape(S5_G, L * S5_GRP, 2 * S5_P)
    ca_re = c_re[None] * pw_re[1:, :, None, :] - c_im[None] * pw_im[1:, :, None, :]
    ca_im = c_re[None] * pw_im[1:, :, None, :] + c_im[None] * pw_re[1:, :, None, :]
    v = jnp.concatenate([ca_re, -ca_im], axis=3)
    v = v.transpose(1, 3, 0, 2).reshape(S5_G, 2 * S5_P, L * S5_GRP)
    return kern.astype(BF16), w.astype(BF16), v.astype(BF16), apow


def _s5_chunk_kernel(u_ref, kt_ref, wc_ref, vc_ref, a1_ref, a2_ref, y_ref, x_ref, kbig, wbig, vbig, *, n_chunks):
    L = S5_L
    GB = S5_GB
    W2 = 2 * S5_P

    @pl.when(pl.program_id(1) == 0)
    def _():
        for s in range(L):
            kbig[s * LANES:(s + 1) * LANES, :] = kt_ref[0, :, (L - 1 - s) * LANES:(2 * L - 1 - s) * LANES]
        qrow = (lax.broadcasted_iota(I32, (L * LANES, W2), 0) // S5_GRP) % GB
        qcol = (lax.broadcasted_iota(I32, (W2, L * LANES), 1) // S5_GRP) % GB
        zero = jnp.zeros((), BF16)
        for r in range(GB):
            wbig[:, r * W2:(r + 1) * W2] = jnp.where(qrow == r, wc_ref[0], zero)
            vbig[r * W2:(r + 1) * W2, :] = jnp.where(qcol == r, vc_ref[0], zero)

    ua = jnp.concatenate([u_ref[pl.ds(s, n_chunks, stride=L), :] for s in range(L)], axis=1).astype(BF16)
    y = jnp.dot(ua, kbig[...], preferred_element_type=F32)
    x = jnp.dot(ua, wbig[...], preferred_element_type=F32)
    swap = lambda a: jnp.concatenate([pltpu.roll(a[:, r * W2:(r + 1) * W2], S5_P, 1) for r in range(GB)], axis=1)
    cidx = lax.broadcasted_iota(I32, (n_chunks, GB * W2), 0)
    for k in range(int(math.log2(n_chunks))):
        sh = 1 << k
        xs = jnp.where(cidx >= sh, pltpu.roll(x, sh, 0), 0.0)
        x = x + a1_ref[0, k:k + 1, :] * xs + a2_ref[0, k:k + 1, :] * swap(xs)
    xin = jnp.where(cidx >= 1, pltpu.roll(x, 1, 0), 0.0).astype(BF16)
    y = y + jnp.dot(xin, vbig[...], preferred_element_type=F32)
    for s in range(L):
        y_ref[pl.ds(s, n_chunks, stride=L), :] = y[:, s * LANES:(s + 1) * LANES]
    x_ref[0] = x[n_chunks - 1:n_chunks, :]


def s5_prompt(u, lam_re, lam_im, log_dt, b_re, b_im, c_re, c_im):
    B, T, _ = u.shape
    L = S5_L
    nch = T // L
    GB = S5_GB
    nb = S5_G // GB
    kern, w, v, apow = _s5_chunk_operators(lam_re, lam_im, log_dt, b_re, b_im, c_re, c_im)
    nlev = int(math.log2(nch))
    ap = [apow(float(L * (1 << k))) for k in range(nlev)]
    apr = jnp.stack([p[0] for p in ap], axis=1)
    api = jnp.stack([p[1] for p in ap], axis=1)
    slab = lambda a: a.reshape(nb, GB, nlev, 2 * S5_P).transpose(0, 2, 1, 3).reshape(nb, nlev, GB * 2 * S5_P)
    a1 = slab(jnp.concatenate([apr, apr], axis=-1))
    a2 = slab(jnp.concatenate([-api, api], axis=-1))
    lag0 = kern.transpose(1, 3, 0, 2)
    eye = jnp.eye(GB, dtype=BF16)
    bd = jnp.einsum('jqitc,qr->jqitrc', lag0.reshape(nb, GB, S5_GRP, L, S5_GRP), eye).reshape(nb, LANES, L * LANES)
    kt = jnp.pad(bd, ((0, 0), (0, 0), ((L - 1) * LANES, 0)))
    wc = w.reshape(nb, GB, L, S5_GRP, 2 * S5_P).transpose(0, 2, 1, 3, 4).reshape(nb, L * LANES, 2 * S5_P)
    vc = v.reshape(nb, GB, 2 * S5_P, L, S5_GRP).transpose(0, 2, 3, 1, 4).reshape(nb, 2 * S5_P, L * LANES)
    W = GB * 2 * S5_P
    wmap = lambda j, b: (j, 0, 0)
    y, x = pl.pallas_call(
        functools.partial(_s5_chunk_kernel, n_chunks=nch), grid=(nb, B),
        in_specs=[pl.BlockSpec((T, LANES), lambda j, b: (b, j)),
                  pl.BlockSpec((1, LANES, (2 * L - 1) * LANES), wmap), pl.BlockSpec((1, L * LANES, 2 * S5_P), wmap),
                  pl.BlockSpec((1, 2 * S5_P, L * LANES), wmap),
                  pl.BlockSpec((1, nlev, W), wmap), pl.BlockSpec((1, nlev, W), wmap)],
        out_specs=[pl.BlockSpec((T, LANES), lambda j, b: (b, j)), pl.BlockSpec((1, 1, W), lambda j, b: (b, 0, j))],
        out_shape=[jax.ShapeDtypeStruct((B * T, S5_CH), F32), jax.ShapeDtypeStruct((B, 1, S5_G * 2 * S5_P), F32)],
        scratch_shapes=[pltpu.VMEM((L * LANES, L * LANES), BF16), pltpu.VMEM((L * LANES, W), BF16),
                        pltpu.VMEM((W, L * LANES), BF16)],
        compiler_params=_cp(("parallel", "arbitrary")), name="s5_chunk_scan")(
            u.reshape(B * T, S5_CH), kt, wc, vc, a1, a2)
    x = x.reshape(B, S5_G, 2, S5_P)
    return y, x[:, :, 0], x[:, :, 1]


def _gelu_tanh(x):
    return 0.5 * x * (1.0 + jnp.tanh(math.sqrt(2.0 / math.pi) * (x + 0.044715 * (x * x * x))))


def _s5_glu_kernel(y_ref, u_ref, d_ref, w_ref, b_ref, o_ref):
    h = _gelu_tanh(y_ref[...] + d_ref[...] * u_ref[...])
    gate = jnp.dot(h.astype(BF16), w_ref[...], preferred_element_type=F32) + b_ref[...]
    o_ref[...] = (h * (1.0 / (1.0 + jnp.exp(-gate)))).astype(o_ref.dtype)


def s5_glu(y, u, d, w_glu, b_glu, tm):
    M, C = y.shape
    row = lambda i: (i, 0)
    fix = lambda i: (0, 0)
    return pl.pallas_call(
        _s5_glu_kernel, grid=(M // tm,),
        in_specs=[pl.BlockSpec((tm, C), row), pl.BlockSpec((tm, C), row), pl.BlockSpec((1, C), fix),
                  pl.BlockSpec((C, C), fix), pl.BlockSpec((1, C), fix)],
        out_specs=pl.BlockSpec((tm, C), row),
        out_shape=jax.ShapeDtypeStruct((M, C), BF16),
        compiler_params=_cp(("parallel",)), name="s5_glu")(y, u, d, w_glu, b_glu)


S5_GB = 8


def _s5_step_kernel(u_ref, x0r_ref, x0i_ref, ar_ref, ai_ref, br_ref, bi_ref, cr_ref, ci_ref, y_ref, xr_ref, xi_ref):
    u = u_ref[...].astype(BF16)
    ar, ai = ar_ref[...], ai_ref[...]
    x0r, x0i = x0r_ref[...], x0i_ref[...]
    xr = ar * x0r - ai * x0i + jnp.dot(u, br_ref[0], preferred_element_type=F32)
    xi = ar * x0i + ai * x0r + jnp.dot(u, bi_ref[0], preferred_element_type=F32)
    xr_ref[...] = xr
    xi_ref[...] = xi
    y_ref[...] = (jnp.dot(xr.astype(BF16), cr_ref[0], preferred_element_type=F32)
                  - jnp.dot(xi.astype(BF16), ci_ref[0], preferred_element_type=F32))


def s5_sample(u, x0_re, x0_im, lam_re, lam_im, log_dt, b_re, b_im, c_re, c_im):
    N = u.shape[0]
    a_re, a_im, bb_re, bb_im, _ = _s5_params(lam_re, lam_im, log_dt, b_re, b_im, c_re, c_im)
    nb = S5_G // S5_GB
    eye = jnp.eye(S5_GB, dtype=F32)
    bd = lambda t: jnp.einsum('jqpc,qr->jqcrp', t.reshape(nb, S5_GB, S5_P, S5_GRP), eye).reshape(
        nb, S5_GB * S5_GRP, S5_GB * S5_P).astype(BF16)
    cd = lambda t: jnp.einsum('jqcp,qr->jrpqc', t.reshape(nb, S5_GB, S5_GRP, S5_P), eye).reshape(
        nb, S5_GB * S5_P, S5_GB * S5_GRP).astype(BF16)
    wu, ws = S5_GB * S5_GRP, S5_GB * S5_P
    col = lambda j: (0, j)
    blk = lambda j: (j, 0, 0)
    y, xr, xi = pl.pallas_call(
        _s5_step_kernel, grid=(nb,),
        in_specs=[pl.BlockSpec((N, wu), col), pl.BlockSpec((N, ws), col), pl.BlockSpec((N, ws), col),
                  pl.BlockSpec((1, ws), col), pl.BlockSpec((1, ws), col),
                  pl.BlockSpec((1, wu, ws), blk), pl.BlockSpec((1, wu, ws), blk),
                  pl.BlockSpec((1, ws, wu), blk), pl.BlockSpec((1, ws, wu), blk)],
        out_specs=[pl.BlockSpec((N, wu), col), pl.BlockSpec((N, ws), col), pl.BlockSpec((N, ws), col)],
        out_shape=[jax.ShapeDtypeStruct((N, S5_CH), F32), jax.ShapeDtypeStruct((N, S5_G * S5_P), F32),
                   jax.ShapeDtypeStruct((N, S5_G * S5_P), F32)],
        compiler_params=_cp(("parallel",)), name="s5_step")(
            u, x0_re.reshape(N, -1), x0_im.reshape(N, -1), a_re.reshape(1, -1), a_im.reshape(1, -1),
            bd(bb_re), bd(bb_im), cd(c_re), cd(c_im))
    return y, xr.reshape(N, S5_G, S5_P), xi.reshape(N, S5_G, S5_P)


def _fetch(pt_ref, n, n_chunks, pages_per_chunk, slot, copies, sem, wait, used_ref=None):
    b = n // n_chunks
    c = n % n_chunks

    def body(p, carry):
        def go():
            page = 0 if wait else pt_ref[b, c * pages_per_chunk + p]
            for src, dst in copies:
                cp = pltpu.make_async_copy(src(page), dst(slot, p), sem.at[slot])
                if wait:
                    cp.wait()
                else:
                    cp.start()

        if used_ref is None:
            go()
        else:
            pl.when(used_ref[b, c * pages_per_chunk + p] != 0)(go)
        return carry

    lax.fori_loop(0, pages_per_chunk, body, 0)


def _pipeline_pages(pt_ref, n_chunks, pages_per_chunk, copies, sem, used_ref=None):
    n = pl.program_id(0)
    slot = n % 2

    @pl.when(n == 0)
    def _():
        _fetch(pt_ref, n, n_chunks, pages_per_chunk, slot, copies, sem, False, used_ref)

    @pl.when(n + 1 < pl.num_programs(0))
    def _():
        _fetch(pt_ref, n + 1, n_chunks, pages_per_chunk, 1 - slot, copies, sem, False, used_ref)

    _fetch(pt_ref, n, n_chunks, pages_per_chunk, slot, copies, sem, True, used_ref)
    return slot


def _softmax_update(s, v, m_ref, l_ref, acc_ref, idx):
    m_prev = m_ref[idx]
    m_new = jnp.maximum(m_prev, jnp.max(s, axis=-1, keepdims=True))
    alpha = jnp.exp2(m_prev - m_new)
    p = jnp.exp2(s - m_new)
    l_ref[idx] = alpha * l_ref[idx] + jnp.sum(p, axis=-1, keepdims=True)
    acc_ref[idx] = alpha * acc_ref[idx] + jnp.dot(p.astype(BF16), v, preferred_element_type=F32)
    m_ref[idx] = m_new


def _softmax_finish(s_new, v_new, m_ref, l_ref, acc_ref, idx):
    m_prev = m_ref[idx]
    m_new = jnp.maximum(m_prev, s_new)
    alpha = jnp.exp2(m_prev - m_new)
    p = jnp.exp2(s_new - m_new)
    return (alpha * acc_ref[idx] + p * v_new) / (alpha * l_ref[idx] + p)


def _init_softmax(c, m_ref, l_ref, acc_ref):
    @pl.when(c == 0)
    def _():
        m_ref[...] = jnp.full_like(m_ref, NEG)
        l_ref[...] = jnp.zeros_like(l_ref)
        acc_ref[...] = jnp.zeros_like(acc_ref)


def _mla_decode_kernel(pt_ref, ql_ref, qr_ref, ln_ref, kn_ref, clat, ckr, o_ref,
                       latbuf, krbuf, sem, m_ref, l_ref, acc_ref, *, n_chunks, ppc, li):
    copies = [(lambda pg: clat.at[li, pg], lambda s, p: latbuf.at[s, p]),
              (lambda pg: ckr.at[li, pg], lambda s, p: krbuf.at[s, :, pl.ds(pl.multiple_of(p * PAGE, PAGE), PAGE)])]
    slot = _pipeline_pages(pt_ref, n_chunks, ppc, copies, sem)
    c = pl.program_id(0) % n_chunks
    _init_softmax(c, m_ref, l_ref, acc_ref)
    L = ppc * PAGE
    klat = latbuf[slot].reshape(L, MLA_LAT).astype(BF16)
    kkr_t = krbuf[slot].astype(BF16)
    ql = ql_ref[0]
    qr = qr_ref[0]
    s = _nt(ql.astype(BF16), klat) + jnp.dot(qr.astype(BF16), kkr_t, preferred_element_type=F32)
    _softmax_update(s, klat, m_ref, l_ref, acc_ref, 0)

    @pl.when(c == n_chunks - 1)
    def _():
        ln = ln_ref[0]
        s_new = jnp.sum(ql * ln, axis=-1, keepdims=True) + jnp.sum(qr * kn_ref[0], axis=-1, keepdims=True)
        o_ref[0] = _softmax_finish(s_new, ln, m_ref, l_ref, acc_ref, 0)


def mla_decode(page_table, q_lat, q_rope, lat_new, kr_new, cache_lat, cache_kr_t, li, ppc):
    N, n_pages = page_table.shape
    n_chunks = n_pages // ppc
    row = lambda n, pt: (n // n_chunks, 0, 0)
    grid_spec = pltpu.PrefetchScalarGridSpec(
        num_scalar_prefetch=1, grid=(N * n_chunks,),
        in_specs=[pl.BlockSpec((1, MLA_H, MLA_LAT), row), pl.BlockSpec((1, MLA_H, MLA_ROPE), row),
                  pl.BlockSpec((1, 1, MLA_LAT), row), pl.BlockSpec((1, 1, MLA_ROPE), row),
                  pl.BlockSpec(memory_space=pl.ANY), pl.BlockSpec(memory_space=pl.ANY)],
        out_specs=pl.BlockSpec((1, MLA_H, MLA_LAT), row),
        scratch_shapes=[pltpu.VMEM((2, ppc, PAGE, MLA_LAT), F32), pltpu.VMEM((2, MLA_ROPE, ppc * PAGE), F32),
                        pltpu.SemaphoreType.DMA((2,)),
                        pltpu.VMEM((1, MLA_H, 1), F32), pltpu.VMEM((1, MLA_H, 1), F32), pltpu.VMEM((1, MLA_H, MLA_LAT), F32)])
    return pl.pallas_call(
        functools.partial(_mla_decode_kernel, n_chunks=n_chunks, ppc=ppc, li=li), grid_spec=grid_spec,
        out_shape=jax.ShapeDtypeStruct((N, MLA_H, MLA_LAT), F32),
        compiler_params=_cp(("arbitrary",)), name="mla_decode")(page_table, q_lat, q_rope, lat_new, kr_new, cache_lat, cache_kr_t)


def _gqa_decode_kernel(*refs, n_chunks, ppc, li, G, R, D, bias_mode):
    pt_ref, used_ref = refs[0], None
    if bias_mode == 'mask':
        used_ref = refs[1]
        refs = refs[1:]
    q_ref, kn_ref, vn_ref = refs[1:4]
    if bias_mode == 'mask':
        bp_ref, bn_ref = refs[4:6]
    else:
        sfx_ref, cn_ref = refs[4:6]
    ck, cv, o_ref, kbuf, vbuf, sem, m_ref, l_ref, acc_ref = refs[6:]
    if used_ref is not None:
        @pl.when(pl.program_id(0) == 0)
        def _():
            kbuf[...] = jnp.zeros_like(kbuf)
            vbuf[...] = jnp.zeros_like(vbuf)

    copies = []
    for g in range(G):
        copies.append((lambda pg, g=g: ck.at[li, pg, :, g, :], lambda s, p, g=g: kbuf.at[s, g, p]))
        copies.append((lambda pg, g=g: cv.at[li, pg, :, g, :], lambda s, p, g=g: vbuf.at[s, g, p]))
    slot = _pipeline_pages(pt_ref, n_chunks, ppc, copies, sem, used_ref)
    c = pl.program_id(0) % n_chunks
    _init_softmax(c, m_ref, l_ref, acc_ref)
    L = ppc * PAGE
    q = q_ref[0]
    for g in range(G):
        k = kbuf[slot, g].reshape(L, D).astype(BF16)
        v = vbuf[slot, g].reshape(L, D).astype(BF16)
        s = _nt(q[g * R:(g + 1) * R].astype(BF16), k)
        if bias_mode == 'mask':
            s = s + bp_ref[0]
        else:
            sfx = jnp.concatenate([sfx_ref[0, g * R:(g + 1) * R, p, :] for p in range(ppc)], axis=1)
            s = s + (sfx + cn_ref[0, g * R:(g + 1) * R, :]) * LOG2E
        _softmax_update(s, v, m_ref, l_ref, acc_ref, g)

    @pl.when(c == n_chunks - 1)
    def _():
        for g in range(G):
            qg = q[g * R:(g + 1) * R]
            s_new = jnp.sum(qg * kn_ref[0, g:g + 1, :], axis=-1, keepdims=True)
            if bias_mode == 'mask':
                s_new = s_new + bn_ref[0, :, 0:1]
            o_ref[0, g * R:(g + 1) * R, :] = _softmax_finish(s_new, vn_ref[0, g:g + 1, :], m_ref, l_ref, acc_ref, g)


def gqa_decode(page_table, q, k_new, v_new, bias_a, bias_b, cache_k, cache_v, li, ppc, bias_mode, page_used=None):
    N, n_pages = page_table.shape
    n_chunks = n_pages // ppc
    H, D = q.shape[1:]
    G = k_new.shape[1]
    R = H // G
    prefetch = (page_table, page_used) if bias_mode == 'mask' else (page_table,)
    row = lambda n, *pf: (n // n_chunks, 0, 0)
    if bias_mode == 'mask':
        bias_specs = [pl.BlockSpec((1, 1, ppc * PAGE), lambda n, *pf: (n // n_chunks, 0, n % n_chunks)),
                      pl.BlockSpec((1, 1, 128), row)]
    else:
        bias_specs = [pl.BlockSpec((1, H, ppc, PAGE), lambda n, *pf: (n // n_chunks, 0, n % n_chunks, 0)),
                      pl.BlockSpec((1, H, 1), row)]
    grid_spec = pltpu.PrefetchScalarGridSpec(
        num_scalar_prefetch=len(prefetch), grid=(N * n_chunks,),
        in_specs=[pl.BlockSpec((1, H, D), row), pl.BlockSpec((1, G, D), row), pl.BlockSpec((1, G, D), row)]
        + bias_specs + [pl.BlockSpec(memory_space=pl.ANY), pl.BlockSpec(memory_space=pl.ANY)],
        out_specs=pl.BlockSpec((1, H, D), row),
        scratch_shapes=[pltpu.VMEM((2, G, ppc, PAGE, D), F32), pltpu.VMEM((2, G, ppc, PAGE, D), F32),
                        pltpu.SemaphoreType.DMA((2,)),
                        pltpu.VMEM((G, R, 1), F32), pltpu.VMEM((G, R, 1), F32), pltpu.VMEM((G, R, D), F32)])
    kern = functools.partial(_gqa_decode_kernel, n_chunks=n_chunks, ppc=ppc, li=li, G=G, R=R, D=D, bias_mode=bias_mode)
    return pl.pallas_call(
        kern, grid_spec=grid_spec, out_shape=jax.ShapeDtypeStruct((N, H, D), F32),
        compiler_params=_cp(("arbitrary",)), name="gqa_decode_" + bias_mode)(
            *prefetch, q, k_new, v_new, bias_a, bias_b, cache_k, cache_v)


def _dsa_score_kernel(pt_ref, qi_ref, wi_ref, cidx, o_ref, kbuf, sem, *, n_chunks, ppc, li):
    copies = [(lambda pg: cidx.at[li, pg], lambda s, p: kbuf.at[s, :, pl.ds(pl.multiple_of(p * PAGE, PAGE), PAGE)])]
    slot = _pipeline_pages(pt_ref, n_chunks, ppc, copies, sem)
    k_t = kbuf[slot].astype(BF16)
    d = jnp.maximum(jnp.dot(qi_ref[0].astype(BF16), k_t, preferred_element_type=F32), 0.0)
    o_ref[0] = jnp.sum(wi_ref[0] * d, axis=0, keepdims=True)


def dsa_sample_scores(page_table, q_idx, w_idx, cache_idx_t, li, ppc):
    N, n_pages = page_table.shape
    n_chunks = n_pages // ppc
    row = lambda n, pt: (n // n_chunks, 0, 0)
    grid_spec = pltpu.PrefetchScalarGridSpec(
        num_scalar_prefetch=1, grid=(N * n_chunks,),
        in_specs=[pl.BlockSpec((1, IDX_H, IDX_D), row), pl.BlockSpec((1, IDX_H, 1), row), pl.BlockSpec(memory_space=pl.ANY)],
        out_specs=pl.BlockSpec((1, 1, ppc * PAGE), lambda n, pt: (n // n_chunks, 0, n % n_chunks)),
        scratch_shapes=[pltpu.VMEM((2, IDX_D, ppc * PAGE), F32), pltpu.SemaphoreType.DMA((2,))])
    return pl.pallas_call(
        functools.partial(_dsa_score_kernel, n_chunks=n_chunks, ppc=ppc, li=li), grid_spec=grid_spec,
        out_shape=jax.ShapeDtypeStruct((N, 1, n_pages * PAGE), F32),
        compiler_params=_cp(("arbitrary",)), name="dsa_sample_scores")(page_table, q_idx, w_idx, cache_idx_t)


def _dsa_select_kernel(sp_ref, qi_ref, wi_ref, kin_ref, bp_ref, bn_ref, used_ref, key_ref, j_ref, *, P):
    rows = sp_ref.shape[0]
    q = qi_ref[...]
    kn = kin_ref[...]
    w = wi_ref[...]
    s_new = jnp.zeros((rows, 1), F32)
    for h in range(IDX_H):
        d = jnp.sum(q[:, h * IDX_D:(h + 1) * IDX_D].astype(BF16).astype(F32) * kn.astype(BF16).astype(F32),
                    axis=-1, keepdims=True)
        s_new = s_new + w[:, h:h + 1] * jnp.maximum(d, 0.0)
    lane = lax.broadcasted_iota(I32, (rows, 128), 1)
    key_ref[:, :P] = _sortable_key(sp_ref[...])
    key_ref[:, P:] = jnp.where(lane == 0, _sortable_key(jnp.broadcast_to(s_new, (rows, 128))), INT_MIN)
    col = lax.broadcasted_iota(I32, (rows, P + 128), 1)
    bias = _topk_bias(key_ref, col <= P, DSA_TOPK, j_ref, int(math.log2(P)) + 1)
    bp_ref[...] = bias[:, :P]
    bn_ref[...] = bias[:, P:]
    n_pages = P // PAGE
    pos_page = lax.broadcasted_iota(I32, (P, n_pages), 0) // PAGE
    ind = jnp.where(pos_page == lax.broadcasted_iota(I32, (P, n_pages), 1), 1.0, 0.0).astype(BF16)
    picked = jnp.where(bias[:, :P] == 0.0, 1.0, 0.0).astype(BF16)
    used_ref[...] = (jnp.dot(picked, ind, preferred_element_type=F32) > 0.0).astype(I32)


def dsa_sample_select(scores_past, q_idx, w_idx, k_idx_new, tr):
    N, P = scores_past.shape
    row = lambda i: (i, 0)
    return pl.pallas_call(
        functools.partial(_dsa_select_kernel, P=P), grid=(N // tr,),
        in_specs=[pl.BlockSpec((tr, P), row), pl.BlockSpec((tr, IDX_H * IDX_D), row), pl.BlockSpec((tr, 128), row),
                  pl.BlockSpec((tr, IDX_D), row)],
        out_specs=[pl.BlockSpec((tr, P), row), pl.BlockSpec((tr, 128), row), pl.BlockSpec((tr, P // PAGE), row)],
        out_shape=[jax.ShapeDtypeStruct((N, P), F32), jax.ShapeDtypeStruct((N, 128), F32),
                   jax.ShapeDtypeStruct((N, P // PAGE), I32)],
        scratch_shapes=[pltpu.VMEM((tr, P + 128), I32), pltpu.VMEM((tr, 1), I32)],
        compiler_params=_cp(("parallel",)), name="dsa_sample_select")(scores_past, q_idx, w_idx, k_idx_new)


def _dot_f32_by_01(x, m, x_is_lhs):
    a = x.astype(BF16)
    r = x - a.astype(F32)
    b = r.astype(BF16)
    c = (r - b.astype(F32)).astype(BF16)
    d = (lambda p: jnp.dot(p, m, preferred_element_type=F32)) if x_is_lhs else (
        lambda p: jnp.dot(m, p, preferred_element_type=F32))
    return d(a) + d(b) + d(c)


def _fox_suffix_kernel(pt_ref, clf, o_ref, buf, sem, *, n_pages, li):
    copies = [(lambda pg: clf.at[li, pg], lambda s, p: buf.at[s, :, p])]
    slot = _pipeline_pages(pt_ref, 1, n_pages, copies, sem)
    i0 = lax.broadcasted_iota(I32, (PAGE, PAGE), 0)
    i1 = lax.broadcasted_iota(I32, (PAGE, PAGE), 1)
    later_pos = jnp.where(i0 > i1, 1.0, 0.0).astype(BF16)
    p0 = lax.broadcasted_iota(I32, (n_pages, n_pages), 0)
    p1 = lax.broadcasted_iota(I32, (n_pages, n_pages), 1)
    later_page = jnp.where(p1 > p0, 1.0, 0.0).astype(BF16)
    for h in range(FOX_H):
        lf = buf[slot, h]
        ins = _dot_f32_by_01(lf, later_pos, True)
        tot = jnp.broadcast_to(ins[:, 0:1] + lf[:, 0:1], (n_pages, PAGE))
        o_ref[0, h] = ins + _dot_f32_by_01(tot, later_page, False)


def fox_suffix(page_table, cache_logf_t, li):
    N, n_pages = page_table.shape
    grid_spec = pltpu.PrefetchScalarGridSpec(
        num_scalar_prefetch=1, grid=(N,),
        in_specs=[pl.BlockSpec(memory_space=pl.ANY)],
        out_specs=pl.BlockSpec((1, FOX_H, n_pages, PAGE), lambda n, pt: (n, 0, 0, 0)),
        scratch_shapes=[pltpu.VMEM((2, FOX_H, n_pages, PAGE), F32), pltpu.SemaphoreType.DMA((2,))])
    return pl.pallas_call(
        functools.partial(_fox_suffix_kernel, n_pages=n_pages, li=li), grid_spec=grid_spec,
        out_shape=jax.ShapeDtypeStruct((N, FOX_H, n_pages, PAGE), F32),
        compiler_params=_cp(("arbitrary",)), name="fox_suffix")(page_table, cache_logf_t)


def _per_head_kernel(a_ref, w_ref, o_ref, *, transpose_w):
    a = a_ref[...].astype(BF16)
    w = w_ref[...].astype(BF16)
    o_ref[...] = _nt(a, w) if transpose_w else jnp.dot(a, w, preferred_element_type=F32)


def per_head_matmul(a, w, d_in, d_out, transpose_w):
    N = a.shape[0]
    H = a.shape[1] // d_in
    wb = (d_out, d_in) if transpose_w else (d_in, d_out)
    return pl.pallas_call(
        functools.partial(_per_head_kernel, transpose_w=transpose_w), grid=(H,),
        in_specs=[pl.BlockSpec((N, d_in), lambda h: (0, h)), pl.BlockSpec(wb, lambda h: (0, h))],
        out_specs=pl.BlockSpec((N, d_out), lambda h: (0, h)),
        out_shape=jax.ShapeDtypeStruct((N, H * d_out), F32),
        compiler_params=_cp(("parallel",)), name="per_head_matmul")(a, w)


def _rope_tables(pos):
    half = MLA_ROPE // 2
    inv_freq = ROPE_BASE ** (-jnp.arange(half, dtype=F32) / half)
    ang = pos.astype(F32)[:, None] * inv_freq
    cos, sin = jnp.cos(ang), jnp.sin(ang)
    z = jnp.zeros((pos.shape[0], 128 - MLA_ROPE), F32)
    return jnp.concatenate([cos, cos, z], axis=1), jnp.concatenate([-sin, sin, z], axis=1)


def _pad_cols(a, n):
    return jnp.pad(a, ((0, 0), (0, n - a.shape[1])))


def _even_weights(w_in, g_q_mla, g_krope, w_uk, w_uv):
    D = w_in.shape[0]
    o = 0
    zq = w_in[:, o:o + MLA_H * MLA_QK].reshape(D, MLA_H, MLA_QK); o += MLA_H * MLA_QK
    zq = jnp.pad(zq, ((0, 0), (0, 0), (0, MLA_PAD - MLA_QK))).reshape(D, MLA_H * MLA_PAD)
    lat = w_in[:, o:o + MLA_LAT]; o += MLA_LAT
    kr = w_in[:, o:o + MLA_ROPE]; o += MLA_ROPE
    qd = w_in[:, o:o + DSA_H * DSA_D]; o += DSA_H * DSA_D
    kd = w_in[:, o:o + DSA_G * DSA_D]; o += DSA_G * DSA_D
    vd = w_in[:, o:o + DSA_G * DSA_D]; o += DSA_G * DSA_D
    qi = w_in[:, o:o + IDX_H * IDX_D]; o += IDX_H * IDX_D
    ki = w_in[:, o:o + IDX_D]; o += IDX_D
    wi = w_in[:, o:o + IDX_H]
    w = jnp.concatenate([zq, lat, qd, kd, vd, qi, kr, ki, _pad_cols(wi, 128)], axis=1).astype(BF16)
    gq = _pad_cols(g_q_mla[None, :], MLA_PAD)
    gkr = _pad_cols(g_krope[None, :], 128)
    wk = jnp.zeros((MLA_LAT + 128, MLA_H, MLA_PAD), F32)
    wk = wk.at[:MLA_LAT, :, :MLA_NOPE].set(w_uk)
    wk = wk.at[MLA_LAT:MLA_LAT + MLA_ROPE, :, MLA_NOPE:MLA_QK].set(
        jnp.broadcast_to(jnp.eye(MLA_ROPE, dtype=F32)[:, None, :], (MLA_ROPE, MLA_H, MLA_ROPE)))
    wv = jnp.zeros((MLA_LAT + 128, MLA_H * MLA_V), F32).at[:MLA_LAT].set(w_uv.reshape(MLA_LAT, MLA_H * MLA_V))
    wkv = jnp.concatenate([wk.reshape(MLA_LAT + 128, MLA_H * MLA_PAD), wv], axis=1).astype(BF16)
    return w, gq, gkr, wkv


def _odd_weights(w_in, fox_b_f):
    w = _pad_cols(w_in, O_N).astype(BF16)
    return w, _pad_cols(fox_b_f[None, :], 128)


def kernel(x_prompt, x_sample, cache_mla_latent, cache_mla_krope, cache_dsa_k, cache_dsa_v, cache_dsa_idx_k, cache_fox_k, cache_fox_v, cache_fox_logf, state_s5_re, state_s5_im, page_table, ln_mix_even, w_in_even, g_q_mla, g_latent, g_krope, w_uk, w_uv, g_q_dsa, g_k_dsa, w_out_even, ln_mix_odd, w_in_odd, s5_lam_re, s5_lam_im, s5_log_dt, s5_b_re, s5_b_im, s5_c_re, s5_c_im, s5_d, s5_w_glu, s5_b_glu, g_q_fox, g_k_fox, fox_b_f, w_out_odd, ln_mlp, w_up, w_down):
    B, S, D = x_prompt.shape
    N = x_sample.shape[0]
    n_pages = page_table.shape[1]
    P = n_pages * PAGE
    MP = B * S
    TM = 512
    TA = 512
    TS = 512
    PPC = 64
    yp = x_prompt.reshape(MP, D)
    ys = x_sample.reshape(N, D)
    cos_p, sin_p = _rope_tables(jnp.arange(S))
    cos_s, sin_s = _rope_tables(jnp.full((N,), P, I32))
    outs = {}
    depth = ln_mlp.shape[0]
    for layer in range(depth):
        li = layer // 2
        if layer % 2 == 0:
            w, gq, gkr, wkv = _even_weights(w_in_even[li], g_q_mla[li], g_krope[li], w_uk[li], w_uv[li])
            gains = (gq, g_latent[li][None], gkr, g_q_dsa[li][None], g_k_dsa[li][None])
            w_out = w_out_even[li].astype(BF16)
            z = rms_matmul(yp, ln_mix_even[li][None], w, TM, E_N // 2)
            qm, lat, kr, lk, qd, kd, vd, qi, ki, wi = even_post(z, cos_p, sin_p, *gains, 256, BF16, S // 256)
            kv = matmul(lk, wkv, BF16, TM, wkv.shape[1] // 3)
            r3 = lambda a: a.reshape(B, S, a.shape[1])
            RM = MLA_H // 2
            o_mla = flash_attention(r3(qm), r3(kv), r3(kv), G=2, R=RM, dq=MLA_PAD, dv=MLA_V, T=TA, TS=TS,
                                    shared_kv=False, v_block0=MLA_H * MLA_PAD // (RM * MLA_V))
            wi_t = r3(wi)[:, :, :8].transpose(0, 2, 1)
            mask = dsa_prompt_mask_t(r3(qi), wi_t, r3(ki), 128, TA)
            o_dsa = flash_attention(r3(qd), r3(kd), r3(vd), G=DSA_G, R=DSA_H // DSA_G, dq=DSA_D, dv=DSA_D, T=TA, TS=TS,
                                    mask=mask)
            yp = out_proj(yp, o_mla.reshape(MP, -1), o_dsa.reshape(MP, -1), w_out, TM, D)
            outs.setdefault('lat_p', []).append(lat.reshape(B, S, MLA_LAT))
            outs.setdefault('kr_p', []).append(kr.reshape(B, S, MLA_ROPE))
            outs.setdefault('dk_p', []).append(kd.reshape(B, S, DSA_G, DSA_D))
            outs.setdefault('dv_p', []).append(vd.reshape(B, S, DSA_G, DSA_D))
            outs.setdefault('di_p', []).append(ki.reshape(B, S, IDX_D))
            z = rms_matmul(ys, ln_mix_even[li][None], w, N, E_N // 2)
            qm, lat, kr, lk, qd, kd, vd, qi, ki, wi = even_post(z, cos_s, sin_s, *gains, N, F32, 1)
            qm3 = qm.reshape(N, MLA_H, MLA_PAD)
            q_nope = qm3[:, :, :MLA_NOPE].reshape(N, MLA_H * MLA_NOPE)
            q_rope = qm3[:, :, MLA_NOPE:MLA_QK]
            w_uk2 = w_uk[li].reshape(MLA_LAT, MLA_H * MLA_NOPE)
            q_lat = per_head_matmul(q_nope, w_uk2, MLA_NOPE, MLA_LAT, True).reshape(N, MLA_H, MLA_LAT)
            o_lat = mla_decode(page_table, q_lat, q_rope, lat[:, None, :], kr[:, None, :],
                               cache_mla_latent, jnp.swapaxes(cache_mla_krope, 2, 3), li, PPC)
            o_mla = per_head_matmul(o_lat.reshape(N, MLA_H * MLA_LAT), w_uv[li].reshape(MLA_LAT, MLA_H * MLA_V),
                                    MLA_LAT, MLA_V, False)
            sc = dsa_sample_scores(page_table, qi.reshape(N, IDX_H, IDX_D), wi[:, :IDX_H, None],
                                   jnp.swapaxes(cache_dsa_idx_k, 2, 3), li, PPC)
            bias_p, bias_n, page_used = dsa_sample_select(sc.reshape(N, P), qi, wi, ki, 32)
            o_dsa = gqa_decode(page_table, qd.reshape(N, DSA_H, DSA_D), kd.reshape(N, DSA_G, DSA_D),
                               vd.reshape(N, DSA_G, DSA_D), bias_p[:, None, :], bias_n[:, None, :],
                               cache_dsa_k, cache_dsa_v, li, PPC, 'mask', page_used)
            ys = out_proj(ys, o_mla, o_dsa.reshape(N, -1), w_out, N, 1024)
            outs.setdefault('lat_s', []).append(lat.reshape(N, 1, MLA_LAT))
            outs.setdefault('kr_s', []).append(kr.reshape(N, 1, MLA_ROPE))
            outs.setdefault('dk_s', []).append(kd.reshape(N, 1, DSA_G, DSA_D))
            outs.setdefault('dv_s', []).append(vd.reshape(N, 1, DSA_G, DSA_D))
            outs.setdefault('di_s', []).append(ki.reshape(N, 1, IDX_D))
        else:
            w, bf = _odd_weights(w_in_odd[li], fox_b_f[li])
            s5p = (s5_lam_re[li], s5_lam_im[li], s5_log_dt[li], s5_b_re[li], s5_b_im[li], s5_c_re[li], s5_c_im[li])
            w_glu = s5_w_glu[li].astype(BF16)
            w_out = w_out_odd[li].astype(BF16)
            R = FOX_H // FOX_G
            z = rms_matmul(yp, ln_mix_odd[li][None], w, TM, O_N // 3)
            u, q, k, v, lf, cum = odd_post(z, g_q_fox[li][None], g_k_fox[li][None], bf, 256, BF16, B, True)
            y5, sre, sim = s5_prompt(u.reshape(B, S, S5_CH), *s5p)
            o_s5 = s5_glu(y5, u, s5_d[li][None], w_glu, s5_b_glu[li][None], TM)
            cum3 = cum[:, :FOX_H].reshape(B, S, FOX_G, R)
            cq = jnp.pad(cum3.transpose(0, 2, 1, 3), ((0, 0), (0, 0), (0, 0), (0, 128 - R)))
            ck = jnp.pad(cum3.transpose(0, 2, 3, 1), ((0, 0), (0, 0), (0, 8 - R), (0, 0)))
            r3 = lambda a: a.reshape(B, S, a.shape[1])
            o_fox = flash_attention(r3(q), r3(k), r3(v), G=FOX_G, R=R, dq=FOX_D, dv=FOX_D, T=TA, TS=TS, cq=cq, ck=ck)
            yp = out_proj(yp, o_s5, o_fox.reshape(MP, -1), w_out, TM, D)
            outs.setdefault('fk_p', []).append(k.reshape(B, S, FOX_G, FOX_D))
            outs.setdefault('fv_p', []).append(v.reshape(B, S, FOX_G, FOX_D))
            outs.setdefault('fl_p', []).append(lf[:, :FOX_H].reshape(B, S, FOX_H))
            outs.setdefault('sre_p', []).append(sre)
            outs.setdefault('sim_p', []).append(sim)
            z = rms_matmul(ys, ln_mix_odd[li][None], w, N, O_N // 3)
            u, q, k, v, lf, _ = odd_post(z, g_q_fox[li][None], g_k_fox[li][None], bf, N, F32, 1, False)
            y5, sre, sim = s5_sample(u, state_s5_re[li], state_s5_im[li], *s5p)
            o_s5 = s5_glu(y5, u, s5_d[li][None], w_glu, s5_b_glu[li][None], N)
            sfx = fox_suffix(page_table, jnp.swapaxes(cache_fox_logf, 2, 3), li)
            o_fox = gqa_decode(page_table, q.reshape(N, FOX_H, FOX_D), k.reshape(N, FOX_G, FOX_D),
                               v.reshape(N, FOX_G, FOX_D), sfx, lf[:, :FOX_H, None],
                               cache_fox_k, cache_fox_v, li, PPC, 'forget')
            ys = out_proj(ys, o_s5, o_fox.reshape(N, -1), w_out, N, 1024)
            outs.setdefault('fk_s', []).append(k.reshape(N, 1, FOX_G, FOX_D))
            outs.setdefault('fv_s', []).append(v.reshape(N, 1, FOX_G, FOX_D))
            outs.setdefault('fl_s', []).append(lf[:, :FOX_H].reshape(N, 1, FOX_H))
            outs.setdefault('sre_s', []).append(sre)
            outs.setdefault('sim_s', []).append(sim)
        wu = w_up[layer].astype(BF16)
        wd = w_down[layer].astype(BF16)
        yp = mlp(yp, ln_mlp[layer][None], wu, wd, TM, 512)
        ys = mlp(ys, ln_mlp[layer][None], wu, wd, N, 512)
    order = ('lat_p', 'lat_s', 'kr_p', 'kr_s', 'dk_p', 'dk_s', 'dv_p', 'dv_s', 'di_p', 'di_s',
             'fk_p', 'fk_s', 'fv_p', 'fv_s', 'fl_p', 'fl_s', 'sre_p', 'sre_s', 'sim_p', 'sim_s')
    return (yp.reshape(B, S, D), ys.reshape(N, 1, D)) + tuple(jnp.stack(outs[name]) for name in order)
```

```python
import functools
import math

import numpy as np
import jax
import jax.numpy as jnp
from jax import lax
from jax.experimental import pallas as pl
from jax.experimental.pallas import tpu as pltpu

F32 = jnp.float32
BF16 = jnp.bfloat16
I32 = jnp.int32

LANES = 128
VMEM_LIMIT_BYTES = 56 * 1024 * 1024

D_MODEL = 2048
PAGE = 128
EPS = 1e-6
ROPE_BASE = 10000.0
MLA_H, MLA_NOPE, MLA_ROPE, MLA_V, MLA_LAT = 8, 128, 64, 128, 256
MLA_QK = MLA_NOPE + MLA_ROPE
MLA_PAD = 256
DSA_H, DSA_G, DSA_D = 8, 2, 128
IDX_H, IDX_D = 4, 64
DSA_TOPK = 256
S5_CH, S5_GRP, S5_G, S5_P = 1024, 16, 64, 64
S5_L = 16
FOX_H, FOX_G, FOX_D = 8, 2, 128
D_FF = 4 * D_MODEL
NEG = -1e30
INT_MIN = -(2 ** 31)
LOG2E = math.log2(math.e)

E_Q, E_LAT, E_QD, E_KD, E_VD, E_QI, E_KK, E_WI, E_N = 0, 2048, 2304, 3328, 3584, 3840, 4096, 4224, 4352
O_U, O_Q, O_K, O_V, O_F, O_N = 0, 1024, 2048, 2304, 2560, 2688


def _cp(sem):
    return pltpu.CompilerParams(dimension_semantics=sem, vmem_limit_bytes=VMEM_LIMIT_BYTES)


def _nt(a, b, precision=None):
    return lax.dot_general(a, b, (((1,), (1,)), ((), ())), preferred_element_type=F32, precision=precision)


def _rms_mm_kernel(x_ref, g_ref, w_ref, o_ref, h_ref):
    @pl.when(pl.program_id(1) == 0)
    def _():
        x = x_ref[...]
        ms = jnp.mean(x * x, axis=-1, keepdims=True)
        h_ref[...] = ((x * lax.rsqrt(ms + EPS)) * g_ref[...]).astype(BF16)

    o_ref[...] = jnp.dot(h_ref[...], w_ref[...], preferred_element_type=F32)


def rms_matmul(x, g, w, tm, tn):
    M, K = x.shape
    N = w.shape[1]
    return pl.pallas_call(
        _rms_mm_kernel, grid=(M // tm, N // tn),
        in_specs=[pl.BlockSpec((tm, K), lambda i, j: (i, 0)),
                  pl.BlockSpec((1, K), lambda i, j: (0, 0)),
                  pl.BlockSpec((K, tn), lambda i, j: (0, j))],
        out_specs=pl.BlockSpec((tm, tn), lambda i, j: (i, j)),
        out_shape=jax.ShapeDtypeStruct((M, N), F32),
        scratch_shapes=[pltpu.VMEM((tm, K), BF16)],
        compiler_params=_cp(("parallel", "arbitrary")), name="rms_matmul")(x, g, w)


def _mm_kernel(a_ref, w_ref, o_ref):
    o_ref[...] = jnp.dot(a_ref[...].astype(BF16), w_ref[...], preferred_element_type=F32).astype(o_ref.dtype)


def matmul(a, w, out_dtype, tm, tn):
    M, K = a.shape
    N = w.shape[1]
    return pl.pallas_call(
        _mm_kernel, grid=(M // tm, N // tn),
        in_specs=[pl.BlockSpec((tm, K), lambda i, j: (i, 0)), pl.BlockSpec((K, tn), lambda i, j: (0, j))],
        out_specs=pl.BlockSpec((tm, tn), lambda i, j: (i, j)),
        out_shape=jax.ShapeDtypeStruct((M, N), out_dtype),
        compiler_params=_cp(("parallel", "arbitrary")), name="matmul")(a, w)


def _out_proj_kernel(x_ref, a1_ref, a2_ref, w1_ref, w2_ref, o_ref):
    acc = jnp.dot(a1_ref[...].astype(BF16), w1_ref[...], preferred_element_type=F32)
    acc = acc + jnp.dot(a2_ref[...].astype(BF16), w2_ref[...], preferred_element_type=F32)
    o_ref[...] = x_ref[...] + acc


def out_proj(x, a1, a2, w, tm, tn):
    M, N = x.shape
    K1, K2 = a1.shape[1], a2.shape[1]
    assert K1 == K2 and w.shape[0] == K1 + K2
    return pl.pallas_call(
        _out_proj_kernel, grid=(M // tm, N // tn),
        in_specs=[pl.BlockSpec((tm, tn), lambda i, j: (i, j)),
                  pl.BlockSpec((tm, K1), lambda i, j: (i, 0)),
                  pl.BlockSpec((tm, K2), lambda i, j: (i, 0)),
                  pl.BlockSpec((K1, tn), lambda i, j: (0, j)),
                  pl.BlockSpec((K2, tn), lambda i, j: (1, j))],
        out_specs=pl.BlockSpec((tm, tn), lambda i, j: (i, j)),
        out_shape=jax.ShapeDtypeStruct((M, N), F32),
        compiler_params=_cp(("parallel", "arbitrary")), name="out_proj")(x, a1, a2, w, w)


def _mlp_kernel(x_ref, g_ref, wu_ref, wd_ref, o_ref, h_ref, acc_ref):
    f = pl.program_id(1)

    @pl.when(f == 0)
    def _():
        x = x_ref[...]
        ms = jnp.mean(x * x, axis=-1, keepdims=True)
        h_ref[...] = ((x * lax.rsqrt(ms + EPS)) * g_ref[...]).astype(BF16)
        acc_ref[...] = jnp.zeros_like(acc_ref)

    up = jnp.dot(h_ref[...], wu_ref[...], preferred_element_type=F32)
    act = jnp.square(jnp.maximum(up, 0.0)).astype(BF16)
    acc_ref[...] += jnp.dot(act, wd_ref[...], preferred_element_type=F32)

    @pl.when(f == pl.num_programs(1) - 1)
    def _():
        o_ref[...] = x_ref[...] + acc_ref[...]


def mlp(x, g, w_up, w_down, tm, tf):
    M, D = x.shape
    FF = w_up.shape[1]
    return pl.pallas_call(
        _mlp_kernel, grid=(M // tm, FF // tf),
        in_specs=[pl.BlockSpec((tm, D), lambda i, f: (i, 0)),
                  pl.BlockSpec((1, D), lambda i, f: (0, 0)),
                  pl.BlockSpec((D, tf), lambda i, f: (0, f)),
                  pl.BlockSpec((tf, D), lambda i, f: (f, 0))],
        out_specs=pl.BlockSpec((tm, D), lambda i, f: (i, 0)),
        out_shape=jax.ShapeDtypeStruct((M, D), F32),
        scratch_shapes=[pltpu.VMEM((tm, D), BF16), pltpu.VMEM((tm, D), F32)],
        compiler_params=_cp(("parallel", "arbitrary")), name="mlp")(x, g, w_up, w_down)


def _rope128(r, c, s):
    lane = lax.broadcasted_iota(I32, r.shape, 1)
    partner = jnp.where(lane < 32, pltpu.roll(r, 96, 1), pltpu.roll(r, 32, 1))
    return r * c + partner * s


def _even_post_kernel(z_ref, cos_ref, sin_ref, gq_ref, gl_ref, gkr_ref, gqd_ref, gkd_ref,
                      qm_ref, lat_ref, kr_ref, lk_ref, qd_ref, kd_ref, vd_ref, qi_ref, ki_ref, wi_ref):
    c = cos_ref[...]
    s = sin_ref[...]
    mla_scale = MLA_QK ** -0.5 * LOG2E
    dsa_scale = DSA_D ** -0.5 * LOG2E
    gq = gq_ref[...]
    for h in range(MLA_H):
        x = z_ref[:, E_Q + h * MLA_PAD:E_Q + (h + 1) * MLA_PAD]
        ms = jnp.sum(x * x, axis=-1, keepdims=True) * (1.0 / MLA_QK)
        y = (x * lax.rsqrt(ms + EPS)) * gq
        qm_ref[:, h * MLA_PAD:h * MLA_PAD + 128] = (y[:, :128] * mla_scale).astype(qm_ref.dtype)
        qm_ref[:, h * MLA_PAD + 128:(h + 1) * MLA_PAD] = (_rope128(y[:, 128:], c, s) * mla_scale).astype(qm_ref.dtype)
    x = z_ref[:, E_LAT:E_LAT + MLA_LAT]
    ms = jnp.mean(x * x, axis=-1, keepdims=True)
    lat = (x * lax.rsqrt(ms + EPS)) * gl_ref[...]
    lat_ref[...] = lat
    zk = z_ref[:, E_KK:E_KK + 128]
    lane = lax.broadcasted_iota(I32, zk.shape, 1)
    kr_in = jnp.where(lane < MLA_ROPE, zk, 0.0)
    ms = jnp.sum(kr_in * kr_in, axis=-1, keepdims=True) * (1.0 / MLA_ROPE)
    kr = _rope128((kr_in * lax.rsqrt(ms + EPS)) * gkr_ref[...], c, s)
    kr_ref[...] = kr[:, :MLA_ROPE]
    ki_ref[...] = zk[:, MLA_ROPE:]
    lk_ref[:, :MLA_LAT] = lat.astype(BF16)
    lk_ref[:, MLA_LAT:] = kr.astype(BF16)
    for h in range(DSA_H):
        x = z_ref[:, E_QD + h * DSA_D:E_QD + (h + 1) * DSA_D]
        ms = jnp.mean(x * x, axis=-1, keepdims=True)
        qd_ref[:, h * DSA_D:(h + 1) * DSA_D] = ((x * lax.rsqrt(ms + EPS)) * gqd_ref[...] * dsa_scale).astype(qd_ref.dtype)
    for h in range(DSA_G):
        x = z_ref[:, E_KD + h * DSA_D:E_KD + (h + 1) * DSA_D]
        ms = jnp.mean(x * x, axis=-1, keepdims=True)
        kd_ref[:, h * DSA_D:(h + 1) * DSA_D] = (x * lax.rsqrt(ms + EPS)) * gkd_ref[...]
    vd_ref[...] = z_ref[:, E_VD:E_VD + DSA_G * DSA_D]
    qi_ref[...] = z_ref[:, E_QI:E_QI + IDX_H * IDX_D].astype(qi_ref.dtype)
    wi_ref[...] = z_ref[:, E_WI:E_WI + 128]


def even_post(z, cos, sin, gq, gl, gkr, gqd, gkd, tm, q_dtype, n_pos_blocks):
    M = z.shape[0]
    row = lambda i: (i, 0)
    pos = lambda i: (i % n_pos_blocks, 0)
    fix = lambda i: (0, 0)
    widths = [(MLA_H * MLA_PAD, q_dtype), (MLA_LAT, F32), (MLA_ROPE, F32), (MLA_LAT + 128, BF16),
              (DSA_H * DSA_D, q_dtype), (DSA_G * DSA_D, F32), (DSA_G * DSA_D, F32),
              (IDX_H * IDX_D, q_dtype), (IDX_D, F32), (128, F32)]
    return pl.pallas_call(
        _even_post_kernel, grid=(M // tm,),
        in_specs=[pl.BlockSpec((tm, E_N), row), pl.BlockSpec((tm, 128), pos), pl.BlockSpec((tm, 128), pos),
                  pl.BlockSpec((1, MLA_PAD), fix), pl.BlockSpec((1, MLA_LAT), fix), pl.BlockSpec((1, 128), fix),
                  pl.BlockSpec((1, DSA_D), fix), pl.BlockSpec((1, DSA_D), fix)],
        out_specs=[pl.BlockSpec((tm, w), row) for w, _ in widths],
        out_shape=[jax.ShapeDtypeStruct((M, w), dt) for w, dt in widths],
        compiler_params=_cp(("parallel",)), name="even_post")(z, cos, sin, gq, gl, gkr, gqd, gkd)


def _log_sigmoid(x):
    return -(jnp.maximum(-x, 0.0) + jnp.log1p(jnp.exp(-jnp.abs(x))))


def _odd_post_kernel(z_ref, gq_ref, gk_ref, bf_ref, u_ref, q_ref, k_ref, v_ref, lf_ref, cum_ref, carry_ref, *, cumulative):
    scale = FOX_D ** -0.5 * LOG2E
    u_ref[...] = z_ref[:, O_U:O_U + S5_CH]
    for h in range(FOX_H):
        x = z_ref[:, O_Q + h * FOX_D:O_Q + (h + 1) * FOX_D]
        ms = jnp.mean(x * x, axis=-1, keepdims=True)
        q_ref[:, h * FOX_D:(h + 1) * FOX_D] = ((x * lax.rsqrt(ms + EPS)) * gq_ref[...] * scale).astype(q_ref.dtype)
    for h in range(FOX_G):
        x = z_ref[:, O_K + h * FOX_D:O_K + (h + 1) * FOX_D]
        ms = jnp.mean(x * x, axis=-1, keepdims=True)
        k_ref[:, h * FOX_D:(h + 1) * FOX_D] = (x * lax.rsqrt(ms + EPS)) * gk_ref[...]
    v_ref[...] = z_ref[:, O_V:O_V + FOX_G * FOX_D]
    zf = z_ref[:, O_F:O_F + 128]
    lane = lax.broadcasted_iota(I32, zf.shape, 1)
    lf = jnp.where(lane < FOX_H, _log_sigmoid(zf + bf_ref[...]), 0.0)
    lf_ref[...] = lf
    if cumulative:
        tm = zf.shape[0]

        @pl.when(pl.program_id(1) == 0)
        def _():
            carry_ref[...] = jnp.zeros_like(carry_ref)

        r = lax.broadcasted_iota(I32, (tm, tm), 0)
        cidx = lax.broadcasted_iota(I32, (tm, tm), 1)
        tri = jnp.where(cidx <= r, 1.0, 0.0).astype(F32)
        cum = jnp.dot(tri, lf, preferred_element_type=F32, precision=lax.Precision.HIGHEST) + carry_ref[...]
        cum_ref[...] = cum * LOG2E
        carry_ref[...] = cum[tm - 1:tm, :]
    else:
        cum_ref[...] = lf


def odd_post(z, gq, gk, bf, tm, q_dtype, n_batch, cumulative):
    M = z.shape[0]
    nb = M // tm // n_batch
    row = lambda b, i: (b * nb + i, 0)
    fix = lambda b, i: (0, 0)
    widths = [(S5_CH, F32), (FOX_H * FOX_D, q_dtype), (FOX_G * FOX_D, F32), (FOX_G * FOX_D, F32), (128, F32), (128, F32)]
    return pl.pallas_call(
        functools.partial(_odd_post_kernel, cumulative=cumulative), grid=(n_batch, nb),
        in_specs=[pl.BlockSpec((tm, O_N), row), pl.BlockSpec((1, FOX_D), fix), pl.BlockSpec((1, FOX_D), fix),
                  pl.BlockSpec((1, 128), fix)],
        out_specs=[pl.BlockSpec((tm, w), row) for w, _ in widths],
        out_shape=[jax.ShapeDtypeStruct((M, w), dt) for w, dt in widths],
        scratch_shapes=[pltpu.VMEM((1, 128), F32)],
        compiler_params=_cp(("arbitrary", "arbitrary")), name="odd_post")(z, gq, gk, bf)


def _flash_kernel(qi_ref, ki_ref, *refs, R, dq, dv, T, TS, shared_kv, has_bias, has_mask):
    q_ref, k_ref, v_ref = refs[:3]
    pos = 3
    cq_ref = ck_ref = mask_ref = None
    if has_bias:
        cq_ref, ck_ref = refs[pos:pos + 2]
        pos += 2
    if has_mask:
        mask_ref = refs[pos]
        pos += 1
    o_ref, m_ref, l_ref, acc_ref = refs[pos:pos + 4]
    t = pl.program_id(2)
    qi = qi_ref[t]
    ki = ki_ref[t]

    @pl.when(ki == 0)
    def _():
        m_ref[...] = jnp.full_like(m_ref, NEG)
        l_ref[...] = jnp.zeros_like(l_ref)
        acc_ref[...] = jnp.zeros_like(acc_ref)

    def step(diag):
        for r in range(R):
            kr = 0 if shared_kv else r
            if r == 0 or not shared_kv:
                k = k_ref[0, :, kr * dq:(kr + 1) * dq].astype(BF16)
                v = v_ref[0, :, kr * dv:(kr + 1) * dv].astype(BF16)
            for qs in range(T // TS):
                rows = slice(qs * TS, (qs + 1) * TS)
                ncol = (qs + 1) * TS if diag else T
                s = _nt(q_ref[0, rows, r * dq:(r + 1) * dq], k[:ncol])
                if has_bias:
                    s = s + (cq_ref[0, 0, rows, r:r + 1] - ck_ref[0, 0, r:r + 1, :ncol])
                if has_mask:
                    mask = mask_ref[0, :, 0].reshape(T, T)
                    s = s + mask[rows, :ncol].astype(F32)
                elif diag:
                    row = qs * TS + lax.broadcasted_iota(I32, (TS, ncol), 0)
                    col = lax.broadcasted_iota(I32, (TS, ncol), 1)
                    s = jnp.where(col <= row, s, NEG)
                m_prev = m_ref[r, rows]
                m_new = jnp.maximum(m_prev, jnp.max(s, axis=1, keepdims=True))
                alpha = jnp.exp2(m_prev - m_new)
                ps = [jnp.exp2(s[:, c * LANES:(c + 1) * LANES] - m_new) for c in range(ncol // LANES)]
                l_ref[r, rows] = alpha * l_ref[r, rows] + functools.reduce(lambda a, b: a + b, ps)
                p = jnp.concatenate(ps, axis=1).astype(BF16)
                acc_ref[r, rows] = alpha * acc_ref[r, rows] + jnp.dot(p, v[:ncol], preferred_element_type=F32)
                m_ref[r, rows] = m_new

    @pl.when(ki < qi)
    def _():
        step(False)

    @pl.when(ki == qi)
    def _():
        step(True)
        for r in range(R):
            l_tot = jnp.sum(l_ref[r], axis=1, keepdims=True)
            o_ref[0, :, r * dv:(r + 1) * dv] = (acc_ref[r] / l_tot).astype(o_ref.dtype)


def flash_attention(q, k, v, *, G, R, dq, dv, T, TS, shared_kv=True, cq=None, ck=None, mask=None, k_block0=0, v_block0=0):
    assert dv == LANES
    B, S, _ = q.shape
    n = S // T
    pairs = [(i, j) for i in range(n) for j in range(i + 1)]
    qi_list = jnp.asarray([p[0] for p in pairs], I32)
    ki_list = jnp.asarray([p[1] for p in pairs], I32)
    kw = 1 if shared_kv else R
    qmap = lambda b, g, t, qi, ki: (b, qi[t], g)
    kmap = lambda b, g, t, qi, ki: (b, ki[t], k_block0 + g)
    vmap = lambda b, g, t, qi, ki: (b, ki[t], v_block0 + g)
    in_specs = [pl.BlockSpec((1, T, R * dq), qmap), pl.BlockSpec((1, T, kw * dq), kmap), pl.BlockSpec((1, T, kw * dv), vmap)]
    args = [q, k, v]
    if cq is not None:
        in_specs += [pl.BlockSpec((1, 1, T, 128), lambda b, g, t, qi, ki: (b, g, qi[t], 0)),
                     pl.BlockSpec((1, 1, 8, T), lambda b, g, t, qi, ki: (b, g, 0, ki[t]))]
        args += [cq, ck]
    if mask is not None:
        tq = mask.shape[3]
        assert mask.shape[4] == T and T % tq == 0
        in_specs += [pl.BlockSpec((1, T // tq, 1, tq, T), lambda b, g, t, qi, ki: (b, qi[t], ki[t], 0, 0))]
        args += [mask]
    kern = functools.partial(_flash_kernel, R=R, dq=dq, dv=dv, T=T, TS=TS, shared_kv=shared_kv,
                             has_bias=cq is not None, has_mask=mask is not None)
    grid_spec = pltpu.PrefetchScalarGridSpec(
        num_scalar_prefetch=2, grid=(B, G, len(pairs)), in_specs=in_specs,
        out_specs=pl.BlockSpec((1, T, R * dv), qmap),
        scratch_shapes=[pltpu.VMEM((R, T, LANES), F32), pltpu.VMEM((R, T, LANES), F32), pltpu.VMEM((R, T, dv), F32)])
    return pl.pallas_call(
        kern, grid_spec=grid_spec, out_shape=jax.ShapeDtypeStruct((B, S, G * R * dv), BF16),
        compiler_params=_cp(("parallel", "parallel", "arbitrary")), name="flash_attention")(qi_list, ki_list, *args)


def _sortable_key(score):
    bits = lax.bitcast_convert_type(score, I32)
    key = bits ^ ((bits >> 31) & jnp.int32(0x7FFFFFFF))
    return jnp.where(key == -1, 0, key)


def _count(mask):
    return jnp.sum(jnp.where(mask, 1.0, 0.0), axis=1, keepdims=True)


def _topk_bias(key_ref, valid, k, j_ref, n_idx_bits):
    rows, cols = key_ref.shape
    kf = float(k)

    def kth(i, t):
        cand = t + jnp.left_shift(jnp.int32(1), 31 - i)
        cnt = _count(key_ref[...] >= cand)
        return jnp.where(cnt >= kf, cand, t)

    thr = lax.fori_loop(0, 32, kth, jnp.full((rows, 1), INT_MIN, I32))
    key = key_ref[...]
    gt = key > thr
    eq = (key == thr) & valid
    need = kf - _count(gt)
    n_eq = _count(eq)
    col = lax.broadcasted_iota(I32, (rows, cols), 1)
    j_ref[...] = jnp.full((rows, 1), cols, I32)
    overfull = jnp.max(jnp.where(n_eq > need, 1.0, 0.0))

    @pl.when(overfull > 0.0)
    def _():
        def cut(i, j0):
            cand = j0 + jnp.left_shift(jnp.int32(1), n_idx_bits - 1 - i)
            cnt = _count((key_ref[...] == thr) & valid & (col < cand))
            return jnp.where(cnt < need, cand, j0)

        j_ref[...] = lax.fori_loop(0, n_idx_bits, cut, jnp.zeros((rows, 1), I32))

    sel = valid & (gt | (eq & (col <= j_ref[...])))
    return jnp.where(sel, 0.0, NEG)


def _dsa_mask_t_kernel(qi_ref, wt_ref, ki_ref, bias_ref, key_ref, tie_ref, j_ref, *, tq, cw, S):
    qb = pl.program_id(1)
    n_all = S // cw
    nc = (qb * tq + tq - 1) // cw + 1
    q = qi_ref[0]
    wt = wt_ref[0]
    kf = float(DSA_TOPK)
    krow = lax.broadcasted_iota(I32, (cw, tq), 0)
    qidx = qb * tq + lax.broadcasted_iota(I32, (cw, tq), 1)

    def score_chunk(c, carry):
        kb = ki_ref[0, pl.ds(pl.multiple_of(c * cw, cw), cw), :].astype(BF16)
        sc = jnp.zeros((cw, tq), F32)
        for h in range(IDX_H):
            sc = sc + wt[h:h + 1, :] * jnp.maximum(_nt(kb, q[:, h * IDX_D:(h + 1) * IDX_D]), 0.0)
        key_ref[c] = jnp.where(c * cw + krow <= qidx, _sortable_key(sc), INT_MIN)
        return carry

    lax.fori_loop(0, nc, score_chunk, 0)

    def count(ref, pred):
        def body(c, acc):
            hit = jnp.where(pred(ref[c]), 1.0, 0.0)
            return acc + jnp.sum(hit.reshape(cw // 64, 64, tq), axis=0)
        return jnp.sum(lax.fori_loop(0, nc, body, jnp.zeros((64, tq), F32)), axis=0, keepdims=True)

    def kth(i, t):
        cand = t + jnp.left_shift(jnp.int32(1), 31 - i)
        return jnp.where(count(key_ref, lambda k: k >= cand) >= kf, cand, t)

    thr = lax.fori_loop(0, 32, kth, jnp.full((1, tq), INT_MIN, I32))
    need = kf - count(key_ref, lambda k: k > thr)

    def tie_chunk(c, carry):
        kidx = c * cw + krow
        tie_ref[c] = jnp.where((key_ref[c] == thr) & (kidx <= qidx), kidx, 2 * S)
        return carry

    lax.fori_loop(0, nc, tie_chunk, 0)
    n_eq = count(tie_ref, lambda t: t < 2 * S)
    j_ref[...] = jnp.full((1, tq), S, I32)
    overfull = jnp.max(jnp.where(n_eq > need, 1.0, 0.0))

    @pl.when(overfull > 0.0)
    def _():
        def cut(i, j0):
            cand = j0 + jnp.left_shift(jnp.int32(1), int(math.log2(S)) - 1 - i)
            return jnp.where(count(tie_ref, lambda t: t < cand) < need, cand, j0)

        j_ref[...] = lax.fori_loop(0, int(math.log2(S)), cut, jnp.zeros((1, tq), I32))

    jcut = j_ref[...]

    def write_chunk(c, carry):
        sel = (key_ref[c] > thr) | (tie_ref[c] <= jcut)
        bias_ref[0, 0, c] = jnp.where(sel, 0.0, NEG).T.astype(BF16)
        return carry

    lax.fori_loop(0, nc, write_chunk, 0)

    def fill_chunk(c, carry):
        bias_ref[0, 0, c] = jnp.full((tq, cw), NEG, BF16)
        return carry

    lax.fori_loop(nc, n_all, fill_chunk, 0)


def dsa_prompt_mask_t(q_idx, w_idx_t, k_idx, tq, cw):
    B, S, _ = q_idx.shape
    return pl.pallas_call(
        functools.partial(_dsa_mask_t_kernel, tq=tq, cw=cw, S=S), grid=(B, S // tq),
        in_specs=[pl.BlockSpec((1, tq, IDX_H * IDX_D), lambda b, i: (b, i, 0)),
                  pl.BlockSpec((1, 8, tq), lambda b, i: (b, 0, i)),
                  pl.BlockSpec((1, S, IDX_D), lambda b, i: (b, 0, 0))],
        out_specs=pl.BlockSpec((1, 1, S // cw, tq, cw), lambda b, i: (b, i, 0, 0, 0)),
        out_shape=jax.ShapeDtypeStruct((B, S // tq, S // cw, tq, cw), BF16),
        scratch_shapes=[pltpu.VMEM((S // cw, cw, tq), I32), pltpu.VMEM((S // cw, cw, tq), I32), pltpu.VMEM((1, tq), I32)],
        compiler_params=_cp(("parallel", "parallel")), name="dsa_prompt_mask")(q_idx, w_idx_t, k_idx)


def _s5_params(lam_re, lam_im, log_dt, b_re, b_im, c_re, c_im):
    dt = jnp.exp(log_dt)[:, None]
    def apow(n):
        mag = jnp.exp(lam_re * dt * n)
        return mag * jnp.cos(lam_im * dt * n), mag * jnp.sin(lam_im * dt * n)
    a_re, a_im = apow(1.0)
    den = lam_re * lam_re + lam_im * lam_im
    x, y = a_re - 1.0, a_im
    co_re = (x * lam_re + y * lam_im) / den
    co_im = (y * lam_re - x * lam_im) / den
    bb_re = co_re[..., None] * b_re - co_im[..., None] * b_im
    bb_im = co_re[..., None] * b_im + co_im[..., None] * b_re
    return a_re, a_im, bb_re, bb_im, apow


def _s5_chunk_operators(lam_re, lam_im, log_dt, b_re, b_im, c_re, c_im):
    a_re, a_im, bb_re, bb_im, apow = _s5_params(lam_re, lam_im, log_dt, b_re, b_im, c_re, c_im)
    L = S5_L
    pw = [apow(float(t)) for t in range(L + 1)]
    pw_re = jnp.stack([p[0] for p in pw])
    pw_im = jnp.stack([p[1] for p in pw])
    ab_re = pw_re[..., None] * bb_re[None] - pw_im[..., None] * bb_im[None]
    ab_im = pw_re[..., None] * bb_im[None] + pw_im[..., None] * bb_re[None]
    hp = lax.Precision.HIGHEST
    kern = (jnp.einsum('gcp,tgpd->tgcd', c_re, ab_re[:L], precision=hp)
            - jnp.einsum('gcp,tgpd->tgcd', c_im, ab_im[:L], precision=hp))
    w_re = ab_re[L - 1 - np.arange(L)]
    w_im = ab_im[L - 1 - np.arange(L)]
    w = jnp.concatenate([w_re, w_im], axis=2)
    w = w.transpose(1, 0, 3, 2).reshape(S5_G, L * S5_GRP, 2 * S5_P)
    ca_re = c_re[None] * pw_re[1:, :, None, :] - c_im[None] * pw_im[1:, :, None, :]
    ca_im = c_re[None] * pw_im[1:, :, None, :] + c_im[None] * pw_re[1:, :, None, :]
    v = jnp.concatenate([ca_re, -ca_im], axis=3)
    v = v.transpose(1, 3, 0, 2).reshape(S5_G, 2 * S5_P, L * S5_GRP)
    return kern.astype(BF16), w.astype(BF16), v.astype(BF16), apow


def _s5_chunk_kernel(u_ref, kt_ref, wc_ref, vc_ref, a1_ref, a2_ref, y_ref, x_ref, kbig, wbig, vbig, *, n_chunks):
    L = S5_L
    GB = S5_GB
    W2 = 2 * S5_P

    @pl.when(pl.program_id(1) == 0)
    def _():
        for s in range(L):
            kbig[s * LANES:(s + 1) * LANES, :] = kt_ref[0, :, (L - 1 - s) * LANES:(2 * L - 1 - s) * LANES]
        qrow = (lax.broadcasted_iota(I32, (L * LANES, W2), 0) // S5_GRP) % GB
        qcol = (lax.broadcasted_iota(I32, (W2, L * LANES), 1) // S5_GRP) % GB
        zero = jnp.zeros((), BF16)
        for r in range(GB):
            wbig[:, r * W2:(r + 1) * W2] = jnp.where(qrow == r, wc_ref[0], zero)
            vbig[r * W2:(r + 1) * W2, :] = jnp.where(qcol == r, vc_ref[0], zero)

    ua = jnp.concatenate([u_ref[pl.ds(s, n_chunks, stride=L), :] for s in range(L)], axis=1).astype(BF16)
    y = jnp.dot(ua, kbig[...], preferred_element_type=F32)
    x = jnp.dot(ua, wbig[...], preferred_element_type=F32)
    swap = lambda a: jnp.concatenate([pltpu.roll(a[:, r * W2:(r + 1) * W2], S5_P, 1) for r in range(GB)], axis=1)
    cidx = lax.broadcasted_iota(I32, (n_chunks, GB * W2), 0)
    for k in range(int(math.log2(n_chunks))):
        sh = 1 << k
        xs = jnp.where(cidx >= sh, pltpu.roll(x, sh, 0), 0.0)
        x = x + a1_ref[0, k:k + 1, :] * xs + a2_ref[0, k:k + 1, :] * swap(xs)
    xin = jnp.where(cidx >= 1, pltpu.roll(x, 1, 0), 0.0).astype(BF16)
    y = y + jnp.dot(xin, vbig[...], preferred_element_type=F32)
    for s in range(L):
        y_ref[pl.ds(s, n_chunks, stride=L), :] = y[:, s * LANES:(s + 1) * LANES]
    x_ref[0] = x[n_chunks - 1:n_chunks, :]


def s5_prompt(u, lam_re, lam_im, log_dt, b_re, b_im, c_re, c_im):
    B, T, _ = u.shape
    L = S5_L
    nch = T // L
    GB = S5_GB
    nb = S5_G // GB
    kern, w, v, apow = _s5_chunk_operators(lam_re, lam_im, log_dt, b_re, b_im, c_re, c_im)
    nlev = int(math.log2(nch))
    ap = [apow(float(L * (1 << k))) for k in range(nlev)]
    apr = jnp.stack([p[0] for p in ap], axis=1)
    api = jnp.stack([p[1] for p in ap], axis=1)
    slab = lambda a: a.reshape(nb, GB, nlev, 2 * S5_P).transpose(0, 2, 1, 3).reshape(nb, nlev, GB * 2 * S5_P)
    a1 = slab(jnp.concatenate([apr, apr], axis=-1))
    a2 = slab(jnp.concatenate([-api, api], axis=-1))
    lag0 = kern.transpose(1, 3, 0, 2)
    eye = jnp.eye(GB, dtype=BF16)
    bd = jnp.einsum('jqitc,qr->jqitrc', lag0.reshape(nb, GB, S5_GRP, L, S5_GRP), eye).reshape(nb, LANES, L * LANES)
    kt = jnp.pad(bd, ((0, 0), (0, 0), ((L - 1) * LANES, 0)))
    wc = w.reshape(nb, GB, L, S5_GRP, 2 * S5_P).transpose(0, 2, 1, 3, 4).reshape(nb, L * LANES, 2 * S5_P)
    vc = v.reshape(nb, GB, 2 * S5_P, L, S5_GRP).transpose(0, 2, 3, 1, 4).reshape(nb, 2 * S5_P, L * LANES)
    W = GB * 2 * S5_P
    wmap = lambda j, b: (j, 0, 0)
    y, x = pl.pallas_call(
        functools.partial(_s5_chunk_kernel, n_chunks=nch), grid=(nb, B),
        in_specs=[pl.BlockSpec((T, LANES), lambda j, b: (b, j)),
                  pl.BlockSpec((1, LANES, (2 * L - 1) * LANES), wmap), pl.BlockSpec((1, L * LANES, 2 * S5_P), wmap),
                  pl.BlockSpec((1, 2 * S5_P, L * LANES), wmap),
                  pl.BlockSpec((1, nlev, W), wmap), pl.BlockSpec((1, nlev, W), wmap)],
        out_specs=[pl.BlockSpec((T, LANES), lambda j, b: (b, j)), pl.BlockSpec((1, 1, W), lambda j, b: (b, 0, j))],
        out_shape=[jax.ShapeDtypeStruct((B * T, S5_CH), F32), jax.ShapeDtypeStruct((B, 1, S5_G * 2 * S5_P), F32)],
        scratch_shapes=[pltpu.VMEM((L * LANES, L * LANES), BF16), pltpu.VMEM((L * LANES, W), BF16),
                        pltpu.VMEM((W, L * LANES), BF16)],
        compiler_params=_cp(("parallel", "arbitrary")), name="s5_chunk_scan")(
            u.reshape(B * T, S5_CH), kt, wc, vc, a1, a2)
    x = x.reshape(B, S5_G, 2, S5_P)
    return y, x[:, :, 0], x[:, :, 1]


def _gelu_tanh(x):
    return 0.5 * x * (1.0 + jnp.tanh(math.sqrt(2.0 / math.pi) * (x + 0.044715 * (x * x * x))))


def _s5_glu_kernel(y_ref, u_ref, d_ref, w_ref, b_ref, o_ref):
    h = _gelu_tanh(y_ref[...] + d_ref[...] * u_ref[...])
    gate = jnp.dot(h.astype(BF16), w_ref[...], preferred_element_type=F32) + b_ref[...]
    o_ref[...] = (h * (1.0 / (1.0 + jnp.exp(-gate)))).astype(o_ref.dtype)


def s5_glu(y, u, d, w_glu, b_glu, tm):
    M, C = y.shape
    row = lambda i: (i, 0)
    fix = lambda i: (0, 0)
    return pl.pallas_call(
        _s5_glu_kernel, grid=(M // tm,),
        in_specs=[pl.BlockSpec((tm, C), row), pl.BlockSpec((tm, C), row), pl.BlockSpec((1, C), fix),
                  pl.BlockSpec((C, C), fix), pl.BlockSpec((1, C), fix)],
        out_specs=pl.BlockSpec((tm, C), row),
        out_shape=jax.ShapeDtypeStruct((M, C), BF16),
        compiler_params=_cp(("parallel",)), name="s5_glu")(y, u, d, w_glu, b_glu)


S5_GB = 8


def _s5_step_kernel(u_ref, x0r_ref, x0i_ref, ar_ref, ai_ref, br_ref, bi_ref, cr_ref, ci_ref, y_ref, xr_ref, xi_ref):
    u = u_ref[...].astype(BF16)
    ar, ai = ar_ref[...], ai_ref[...]
    x0r, x0i = x0r_ref[...], x0i_ref[...]
    xr = ar * x0r - ai * x0i + jnp.dot(u, br_ref[0], preferred_element_type=F32)
    xi = ar * x0i + ai * x0r + jnp.dot(u, bi_ref[0], preferred_element_type=F32)
    xr_ref[...] = xr
    xi_ref[...] = xi
    y_ref[...] = (jnp.dot(xr.astype(BF16), cr_ref[0], preferred_element_type=F32)
                  - jnp.dot(xi.astype(BF16), ci_ref[0], preferred_element_type=F32))


def s5_sample(u, x0_re, x0_im, lam_re, lam_im, log_dt, b_re, b_im, c_re, c_im):
    N = u.shape[0]
    a_re, a_im, bb_re, bb_im, _ = _s5_params(lam_re, lam_im, log_dt, b_re, b_im, c_re, c_im)
    nb = S5_G // S5_GB
    eye = jnp.eye(S5_GB, dtype=F32)
    bd = lambda t: jnp.einsum('jqpc,qr->jqcrp', t.reshape(nb, S5_GB, S5_P, S5_GRP), eye).reshape(
        nb, S5_GB * S5_GRP, S5_GB * S5_P).astype(BF16)
    cd = lambda t: jnp.einsum('jqcp,qr->jrpqc', t.reshape(nb, S5_GB, S5_GRP, S5_P), eye).reshape(
        nb, S5_GB * S5_P, S5_GB * S5_GRP).astype(BF16)
    wu, ws = S5_GB * S5_GRP, S5_GB * S5_P
    col = lambda j: (0, j)
    blk = lambda j: (j, 0, 0)
    y, xr, xi = pl.pallas_call(
        _s5_step_kernel, grid=(nb,),
        in_specs=[pl.BlockSpec((N, wu), col), pl.BlockSpec((N, ws), col), pl.BlockSpec((N, ws), col),
                  pl.BlockSpec((1, ws), col), pl.BlockSpec((1, ws), col),
                  pl.BlockSpec((1, wu, ws), blk), pl.BlockSpec((1, wu, ws), blk),
                  pl.BlockSpec((1, ws, wu), blk), pl.BlockSpec((1, ws, wu), blk)],
        out_specs=[pl.BlockSpec((N, wu), col), pl.BlockSpec((N, ws), col), pl.BlockSpec((N, ws), col)],
        out_shape=[jax.ShapeDtypeStruct((N, S5_CH), F32), jax.ShapeDtypeStruct((N, S5_G * S5_P), F32),
                   jax.ShapeDtypeStruct((N, S5_G * S5_P), F32)],
        compiler_params=_cp(("parallel",)), name="s5_step")(
            u, x0_re.reshape(N, -1), x0_im.reshape(N, -1), a_re.reshape(1, -1), a_im.reshape(1, -1),
            bd(bb_re), bd(bb_im), cd(c_re), cd(c_im))
    return y, xr.reshape(N, S5_G, S5_P), xi.reshape(N, S5_G, S5_P)


def _fetch(pt_ref, n, n_chunks, pages_per_chunk, slot, copies, sem, wait, used_ref=None):
    b = n // n_chunks
    c = n % n_chunks

    def body(p, carry):
        def go():
            page = 0 if wait else pt_ref[b, c * pages_per_chunk + p]
            for src, dst in copies:
                cp = pltpu.make_async_copy(src(page), dst(slot, p), sem.at[slot])
                if wait:
                    cp.wait()
                else:
                    cp.start()

        if used_ref is None:
            go()
        else:
            pl.when(used_ref[b, c * pages_per_chunk + p] != 0)(go)
        return carry

    lax.fori_loop(0, pages_per_chunk, body, 0)


def _pipeline_pages(pt_ref, n_chunks, pages_per_chunk, copies, sem, used_ref=None):
    n = pl.program_id(0)
    slot = n % 2

    @pl.when(n == 0)
    def _():
        _fetch(pt_ref, n, n_chunks, pages_per_chunk, slot, copies, sem, False, used_ref)

    @pl.when(n + 1 < pl.num_programs(0))
    def _():
        _fetch(pt_ref, n + 1, n_chunks, pages_per_chunk, 1 - slot, copies, sem, False, used_ref)

    _fetch(pt_ref, n, n_chunks, pages_per_chunk, slot, copies, sem, True, used_ref)
    return slot


def _softmax_update(s, v, m_ref, l_ref, acc_ref, idx):
    m_prev = m_ref[idx]
    m_new = jnp.maximum(m_prev, jnp.max(s, axis=-1, keepdims=True))
    alpha = jnp.exp2(m_prev - m_new)
    p = jnp.exp2(s - m_new)
    l_ref[idx] = alpha * l_ref[idx] + jnp.sum(p, axis=-1, keepdims=True)
    acc_ref[idx] = alpha * acc_ref[idx] + jnp.dot(p.astype(BF16), v, preferred_element_type=F32)
    m_ref[idx] = m_new


def _softmax_finish(s_new, v_new, m_ref, l_ref, acc_ref, idx):
    m_prev = m_ref[idx]
    m_new = jnp.maximum(m_prev, s_new)
    alpha = jnp.exp2(m_prev - m_new)
    p = jnp.exp2(s_new - m_new)
    return (alpha * acc_ref[idx] + p * v_new) / (alpha * l_ref[idx] + p)


def _init_softmax(c, m_ref, l_ref, acc_ref):
    @pl.when(c == 0)
    def _():
        m_ref[...] = jnp.full_like(m_ref, NEG)
        l_ref[...] = jnp.zeros_like(l_ref)
        acc_ref[...] = jnp.zeros_like(acc_ref)


def _mla_decode_kernel(pt_ref, ql_ref, qr_ref, ln_ref, kn_ref, clat, ckr, o_ref,
                       latbuf, krbuf, sem, m_ref, l_ref, acc_ref, *, n_chunks, ppc, li):
    copies = [(lambda pg: clat.at[li, pg], lambda s, p: latbuf.at[s, p]),
              (lambda pg: ckr.at[li, pg], lambda s, p: krbuf.at[s, :, pl.ds(pl.multiple_of(p * PAGE, PAGE), PAGE)])]
    slot = _pipeline_pages(pt_ref, n_chunks, ppc, copies, sem)
    c = pl.program_id(0) % n_chunks
    _init_softmax(c, m_ref, l_ref, acc_ref)
    L = ppc * PAGE
    klat = latbuf[slot].reshape(L, MLA_LAT).astype(BF16)
    kkr_t = krbuf[slot].astype(BF16)
    ql = ql_ref[0]
    qr = qr_ref[0]
    s = _nt(ql.astype(BF16), klat) + jnp.dot(qr.astype(BF16), kkr_t, preferred_element_type=F32)
    _softmax_update(s, klat, m_ref, l_ref, acc_ref, 0)

    @pl.when(c == n_chunks - 1)
    def _():
        ln = ln_ref[0]
        s_new = jnp.sum(ql * ln, axis=-1, keepdims=True) + jnp.sum(qr * kn_ref[0], axis=-1, keepdims=True)
        o_ref[0] = _softmax_finish(s_new, ln, m_ref, l_ref, acc_ref, 0)


def mla_decode(page_table, q_lat, q_rope, lat_new, kr_new, cache_lat, cache_kr_t, li, ppc):
    N, n_pages = page_table.shape
    n_chunks = n_pages // ppc
    row = lambda n, pt: (n // n_chunks, 0, 0)
    grid_spec = pltpu.PrefetchScalarGridSpec(
        num_scalar_prefetch=1, grid=(N * n_chunks,),
        in_specs=[pl.BlockSpec((1, MLA_H, MLA_LAT), row), pl.BlockSpec((1, MLA_H, MLA_ROPE), row),
                  pl.BlockSpec((1, 1, MLA_LAT), row), pl.BlockSpec((1, 1, MLA_ROPE), row),
                  pl.BlockSpec(memory_space=pl.ANY), pl.BlockSpec(memory_space=pl.ANY)],
        out_specs=pl.BlockSpec((1, MLA_H, MLA_LAT), row),
        scratch_shapes=[pltpu.VMEM((2, ppc, PAGE, MLA_LAT), F32), pltpu.VMEM((2, MLA_ROPE, ppc * PAGE), F32),
                        pltpu.SemaphoreType.DMA((2,)),
                        pltpu.VMEM((1, MLA_H, 1), F32), pltpu.VMEM((1, MLA_H, 1), F32), pltpu.VMEM((1, MLA_H, MLA_LAT), F32)])
    return pl.pallas_call(
        functools.partial(_mla_decode_kernel, n_chunks=n_chunks, ppc=ppc, li=li), grid_spec=grid_spec,
        out_shape=jax.ShapeDtypeStruct((N, MLA_H, MLA_LAT), F32),
        compiler_params=_cp(("arbitrary",)), name="mla_decode")(page_table, q_lat, q_rope, lat_new, kr_new, cache_lat, cache_kr_t)


def _gqa_decode_kernel(*refs, n_chunks, ppc, li, G, R, D, bias_mode):
    pt_ref, used_ref = refs[0], None
    if bias_mode == 'mask':
        used_ref = refs[1]
        refs = refs[1:]
    q_ref, kn_ref, vn_ref = refs[1:4]
    if bias_mode == 'mask':
        bp_ref, bn_ref = refs[4:6]
    else:
        sfx_ref, cn_ref = refs[4:6]
    ck, cv, o_ref, kbuf, vbuf, sem, m_ref, l_ref, acc_ref = refs[6:]
    if used_ref is not None:
        @pl.when(pl.program_id(0) == 0)
        def _():
            kbuf[...] = jnp.zeros_like(kbuf)
            vbuf[...] = jnp.zeros_like(vbuf)

    copies = []
    for g in range(G):
        copies.append((lambda pg, g=g: ck.at[li, pg, :, g, :], lambda s, p, g=g: kbuf.at[s, g, p]))
        copies.append((lambda pg, g=g: cv.at[li, pg, :, g, :], lambda s, p, g=g: vbuf.at[s, g, p]))
    slot = _pipeline_pages(pt_ref, n_chunks, ppc, copies, sem, used_ref)
    c = pl.program_id(0) % n_chunks
    _init_softmax(c, m_ref, l_ref, acc_ref)
    L = ppc * PAGE
    q = q_ref[0]
    for g in range(G):
        k = kbuf[slot, g].reshape(L, D).astype(BF16)
        v = vbuf[slot, g].reshape(L, D).astype(BF16)
        s = _nt(q[g * R:(g + 1) * R].astype(BF16), k)
        if bias_mode == 'mask':
            s = s + bp_ref[0]
        else:
            sfx = jnp.concatenate([sfx_ref[0, g * R:(g + 1) * R, p, :] for p in range(ppc)], axis=1)
            s = s + (sfx + cn_ref[0, g * R:(g + 1) * R, :]) * LOG2E
        _softmax_update(s, v, m_ref, l_ref, acc_ref, g)

    @pl.when(c == n_chunks - 1)
    def _():
        for g in range(G):
            qg = q[g * R:(g + 1) * R]
            s_new = jnp.sum(qg * kn_ref[0, g:g + 1, :], axis=-1, keepdims=True)
            if bias_mode == 'mask':
                s_new = s_new + bn_ref[0, :, 0:1]
            o_ref[0, g * R:(g + 1) * R, :] = _softmax_finish(s_new, vn_ref[0, g:g + 1, :], m_ref, l_ref, acc_ref, g)


def gqa_decode(page_table, q, k_new, v_new, bias_a, bias_b, cache_k, cache_v, li, ppc, bias_mode, page_used=None):
    N, n_pages = page_table.shape
    n_chunks = n_pages // ppc
    H, D = q.shape[1:]
    G = k_new.shape[1]
    R = H // G
    prefetch = (page_table, page_used) if bias_mode == 'mask' else (page_table,)
    row = lambda n, *pf: (n // n_chunks, 0, 0)
    if bias_mode == 'mask':
        bias_specs = [pl.BlockSpec((1, 1, ppc * PAGE), lambda n, *pf: (n // n_chunks, 0, n % n_chunks)),
                      pl.BlockSpec((1, 1, 128), row)]
    else:
        bias_specs = [pl.BlockSpec((1, H, ppc, PAGE), lambda n, *pf: (n // n_chunks, 0, n % n_chunks, 0)),
                      pl.BlockSpec((1, H, 1), row)]
    grid_spec = pltpu.PrefetchScalarGridSpec(
        num_scalar_prefetch=len(prefetch), grid=(N * n_chunks,),
        in_specs=[pl.BlockSpec((1, H, D), row), pl.BlockSpec((1, G, D), row), pl.BlockSpec((1, G, D), row)]
        + bias_specs + [pl.BlockSpec(memory_space=pl.ANY), pl.BlockSpec(memory_space=pl.ANY)],
        out_specs=pl.BlockSpec((1, H, D), row),
        scratch_shapes=[pltpu.VMEM((2, G, ppc, PAGE, D), F32), pltpu.VMEM((2, G, ppc, PAGE, D), F32),
                        pltpu.SemaphoreType.DMA((2,)),
                        pltpu.VMEM((G, R, 1), F32), pltpu.VMEM((G, R, 1), F32), pltpu.VMEM((G, R, D), F32)])
    kern = functools.partial(_gqa_decode_kernel, n_chunks=n_chunks, ppc=ppc, li=li, G=G, R=R, D=D, bias_mode=bias_mode)
    return pl.pallas_call(
        kern, grid_spec=grid_spec, out_shape=jax.ShapeDtypeStruct((N, H, D), F32),
        compiler_params=_cp(("arbitrary",)), name="gqa_decode_" + bias_mode)(
            *prefetch, q, k_new, v_new, bias_a, bias_b, cache_k, cache_v)


def _dsa_score_kernel(pt_ref, qi_ref, wi_ref, cidx, o_ref, kbuf, sem, *, n_chunks, ppc, li):
    copies = [(lambda pg: cidx.at[li, pg], lambda s, p: kbuf.at[s, :, pl.ds(pl.multiple_of(p * PAGE, PAGE), PAGE)])]
    slot = _pipeline_pages(pt_ref, n_chunks, ppc, copies, sem)
    k_t = kbuf[slot]
    qt = qi_ref[0]
    w = wi_ref[0]
    score = jnp.zeros((1, k_t.shape[1]), F32)
    for h in range(IDX_H):
        d = jnp.sum(k_t * qt[:, h:h + 1], axis=0, keepdims=True)
        score = score + w[h:h + 1, :] * jnp.maximum(d, 0.0)
    o_ref[0] = score


def dsa_sample_scores(page_table, q_idx, w_idx, cache_idx_t, li, ppc):
    N, n_pages = page_table.shape
    n_chunks = n_pages // ppc
    row = lambda n, pt: (n // n_chunks, 0, 0)
    grid_spec = pltpu.PrefetchScalarGridSpec(
        num_scalar_prefetch=1, grid=(N * n_chunks,),
        in_specs=[pl.BlockSpec((1, IDX_D, IDX_H), row), pl.BlockSpec((1, IDX_H, 1), row), pl.BlockSpec(memory_space=pl.ANY)],
        out_specs=pl.BlockSpec((1, 1, ppc * PAGE), lambda n, pt: (n // n_chunks, 0, n % n_chunks)),
        scratch_shapes=[pltpu.VMEM((2, IDX_D, ppc * PAGE), F32), pltpu.SemaphoreType.DMA((2,))])
    return pl.pallas_call(
        functools.partial(_dsa_score_kernel, n_chunks=n_chunks, ppc=ppc, li=li), grid_spec=grid_spec,
        out_shape=jax.ShapeDtypeStruct((N, 1, n_pages * PAGE), F32),
        compiler_params=_cp(("arbitrary",)), name="dsa_sample_scores")(page_table, q_idx, w_idx, cache_idx_t)


def _dsa_select_kernel(sp_ref, qi_ref, wi_ref, kin_ref, bp_ref, bn_ref, used_ref, key_ref, j_ref, *, P):
    rows = sp_ref.shape[0]
    q = qi_ref[...]
    kn = kin_ref[...]
    w = wi_ref[...]
    s_new = jnp.zeros((rows, 1), F32)
    for h in range(IDX_H):
        d = jnp.sum(q[:, h * IDX_D:(h + 1) * IDX_D].astype(BF16).astype(F32) * kn.astype(BF16).astype(F32),
                    axis=-1, keepdims=True)
        s_new = s_new + w[:, h:h + 1] * jnp.maximum(d, 0.0)
    lane = lax.broadcasted_iota(I32, (rows, 128), 1)
    key_ref[:, :P] = _sortable_key(sp_ref[...])
    key_ref[:, P:] = jnp.where(lane == 0, _sortable_key(jnp.broadcast_to(s_new, (rows, 128))), INT_MIN)
    col = lax.broadcasted_iota(I32, (rows, P + 128), 1)
    bias = _topk_bias(key_ref, col <= P, DSA_TOPK, j_ref, int(math.log2(P)) + 1)
    bp_ref[...] = bias[:, :P]
    bn_ref[...] = bias[:, P:]
    n_pages = P // PAGE
    pos_page = lax.broadcasted_iota(I32, (P, n_pages), 0) // PAGE
    ind = jnp.where(pos_page == lax.broadcasted_iota(I32, (P, n_pages), 1), 1.0, 0.0).astype(BF16)
    picked = jnp.where(bias[:, :P] == 0.0, 1.0, 0.0).astype(BF16)
    used_ref[...] = (jnp.dot(picked, ind, preferred_element_type=F32) > 0.0).astype(I32)


def dsa_sample_select(scores_past, q_idx, w_idx, k_idx_new, tr):
    N, P = scores_past.shape
    row = lambda i: (i, 0)
    return pl.pallas_call(
        functools.partial(_dsa_select_kernel, P=P), grid=(N // tr,),
        in_specs=[pl.BlockSpec((tr, P), row), pl.BlockSpec((tr, IDX_H * IDX_D), row), pl.BlockSpec((tr, 128), row),
                  pl.BlockSpec((tr, IDX_D), row)],
        out_specs=[pl.BlockSpec((tr, P), row), pl.BlockSpec((tr, 128), row), pl.BlockSpec((tr, P // PAGE), row)],
        out_shape=[jax.ShapeDtypeStruct((N, P), F32), jax.ShapeDtypeStruct((N, 128), F32),
                   jax.ShapeDtypeStruct((N, P // PAGE), I32)],
        scratch_shapes=[pltpu.VMEM((tr, P + 128), I32), pltpu.VMEM((tr, 1), I32)],
        compiler_params=_cp(("parallel",)), name="dsa_sample_select")(scores_past, q_idx, w_idx, k_idx_new)


def _dot_f32_by_01(x, m, x_is_lhs):
    a = x.astype(BF16)
    r = x - a.astype(F32)
    b = r.astype(BF16)
    c = (r - b.astype(F32)).astype(BF16)
    d = (lambda p: jnp.dot(p, m, preferred_element_type=F32)) if x_is_lhs else (
        lambda p: jnp.dot(m, p, preferred_element_type=F32))
    return d(a) + d(b) + d(c)


def _fox_suffix_kernel(pt_ref, clf, o_ref, buf, sem, *, n_pages, li):
    copies = [(lambda pg: clf.at[li, pg], lambda s, p: buf.at[s, :, p])]
    slot = _pipeline_pages(pt_ref, 1, n_pages, copies, sem)
    i0 = lax.broadcasted_iota(I32, (PAGE, PAGE), 0)
    i1 = lax.broadcasted_iota(I32, (PAGE, PAGE), 1)
    later_pos = jnp.where(i0 > i1, 1.0, 0.0).astype(BF16)
    p0 = lax.broadcasted_iota(I32, (n_pages, n_pages), 0)
    p1 = lax.broadcasted_iota(I32, (n_pages, n_pages), 1)
    later_page = jnp.where(p1 > p0, 1.0, 0.0).astype(BF16)
    for h in range(FOX_H):
        lf = buf[slot, h]
        ins = _dot_f32_by_01(lf, later_pos, True)
        tot = jnp.broadcast_to(ins[:, 0:1] + lf[:, 0:1], (n_pages, PAGE))
        o_ref[0, h] = ins + _dot_f32_by_01(tot, later_page, False)


def fox_suffix(page_table, cache_logf_t, li):
    N, n_pages = page_table.shape
    grid_spec = pltpu.PrefetchScalarGridSpec(
        num_scalar_prefetch=1, grid=(N,),
        in_specs=[pl.BlockSpec(memory_space=pl.ANY)],
        out_specs=pl.BlockSpec((1, FOX_H, n_pages, PAGE), lambda n, pt: (n, 0, 0, 0)),
        scratch_shapes=[pltpu.VMEM((2, FOX_H, n_pages, PAGE), F32), pltpu.SemaphoreType.DMA((2,))])
    return pl.pallas_call(
        functools.partial(_fox_suffix_kernel, n_pages=n_pages, li=li), grid_spec=grid_spec,
        out_shape=jax.ShapeDtypeStruct((N, FOX_H, n_pages, PAGE), F32),
        compiler_params=_cp(("arbitrary",)), name="fox_suffix")(page_table, cache_logf_t)


def _per_head_kernel(a_ref, w_ref, o_ref, *, transpose_w):
    a = a_ref[...].astype(BF16)
    w = w_ref[...].astype(BF16)
    o_ref[...] = _nt(a, w) if transpose_w else jnp.dot(a, w, preferred_element_type=F32)


def per_head_matmul(a, w, d_in, d_out, transpose_w):
    N = a.shape[0]
    H = a.shape[1] // d_in
    wb = (d_out, d_in) if transpose_w else (d_in, d_out)
    return pl.pallas_call(
        functools.partial(_per_head_kernel, transpose_w=transpose_w), grid=(H,),
        in_specs=[pl.BlockSpec((N, d_in), lambda h: (0, h)), pl.BlockSpec(wb, lambda h: (0, h))],
        out_specs=pl.BlockSpec((N, d_out), lambda h: (0, h)),
        out_shape=jax.ShapeDtypeStruct((N, H * d_out), F32),
        compiler_params=_cp(("parallel",)), name="per_head_matmul")(a, w)


def _rope_tables(pos):
    half = MLA_ROPE // 2
    inv_freq = ROPE_BASE ** (-jnp.arange(half, dtype=F32) / half)
    ang = pos.astype(F32)[:, None] * inv_freq
    cos, sin = jnp.cos(ang), jnp.sin(ang)
    z = jnp.zeros((pos.shape[0], 128 - MLA_ROPE), F32)
    return jnp.concatenate([cos, cos, z], axis=1), jnp.concatenate([-sin, sin, z], axis=1)


def _pad_cols(a, n):
    return jnp.pad(a, ((0, 0), (0, n - a.shape[1])))


def _even_weights(w_in, g_q_mla, g_krope, w_uk, w_uv):
    D = w_in.shape[0]
    o = 0
    zq = w_in[:, o:o + MLA_H * MLA_QK].reshape(D, MLA_H, MLA_QK); o += MLA_H * MLA_QK
    zq = jnp.pad(zq, ((0, 0), (0, 0), (0, MLA_PAD - MLA_QK))).reshape(D, MLA_H * MLA_PAD)
    lat = w_in[:, o:o + MLA_LAT]; o += MLA_LAT
    kr = w_in[:, o:o + MLA_ROPE]; o += MLA_ROPE
    qd = w_in[:, o:o + DSA_H * DSA_D]; o += DSA_H * DSA_D
    kd = w_in[:, o:o + DSA_G * DSA_D]; o += DSA_G * DSA_D
    vd = w_in[:, o:o + DSA_G * DSA_D]; o += DSA_G * DSA_D
    qi = w_in[:, o:o + IDX_H * IDX_D]; o += IDX_H * IDX_D
    ki = w_in[:, o:o + IDX_D]; o += IDX_D
    wi = w_in[:, o:o + IDX_H]
    w = jnp.concatenate([zq, lat, qd, kd, vd, qi, kr, ki, _pad_cols(wi, 128)], axis=1).astype(BF16)
    gq = _pad_cols(g_q_mla[None, :], MLA_PAD)
    gkr = _pad_cols(g_krope[None, :], 128)
    wk = jnp.zeros((MLA_LAT + 128, MLA_H, MLA_PAD), F32)
    wk = wk.at[:MLA_LAT, :, :MLA_NOPE].set(w_uk)
    wk = wk.at[MLA_LAT:MLA_LAT + MLA_ROPE, :, MLA_NOPE:MLA_QK].set(
        jnp.broadcast_to(jnp.eye(MLA_ROPE, dtype=F32)[:, None, :], (MLA_ROPE, MLA_H, MLA_ROPE)))
    wv = jnp.zeros((MLA_LAT + 128, MLA_H * MLA_V), F32).at[:MLA_LAT].set(w_uv.reshape(MLA_LAT, MLA_H * MLA_V))
    wkv = jnp.concatenate([wk.reshape(MLA_LAT + 128, MLA_H * MLA_PAD), wv], axis=1).astype(BF16)
    return w, gq, gkr, wkv


def _odd_weights(w_in, fox_b_f):
    w = _pad_cols(w_in, O_N).astype(BF16)
    return w, _pad_cols(fox_b_f[None, :], 128)


def kernel(x_prompt, x_sample, cache_mla_latent, cache_mla_krope, cache_dsa_k, cache_dsa_v, cache_dsa_idx_k, cache_fox_k, cache_fox_v, cache_fox_logf, state_s5_re, state_s5_im, page_table, ln_mix_even, w_in_even, g_q_mla, g_latent, g_krope, w_uk, w_uv, g_q_dsa, g_k_dsa, w_out_even, ln_mix_odd, w_in_odd, s5_lam_re, s5_lam_im, s5_log_dt, s5_b_re, s5_b_im, s5_c_re, s5_c_im, s5_d, s5_w_glu, s5_b_glu, g_q_fox, g_k_fox, fox_b_f, w_out_odd, ln_mlp, w_up, w_down):
    B, S, D = x_prompt.shape
    N = x_sample.shape[0]
    n_pages = page_table.shape[1]
    P = n_pages * PAGE
    MP = B * S
    TM = 512
    TA = 512
    TS = 512
    PPC = 64
    yp = x_prompt.reshape(MP, D)
    ys = x_sample.reshape(N, D)
    cos_p, sin_p = _rope_tables(jnp.arange(S))
    cos_s, sin_s = _rope_tables(jnp.full((N,), P, I32))
    outs = {}
    depth = ln_mlp.shape[0]
    for layer in range(depth):
        li = layer // 2
        if layer % 2 == 0:
            w, gq, gkr, wkv = _even_weights(w_in_even[li], g_q_mla[li], g_krope[li], w_uk[li], w_uv[li])
            gains = (gq, g_latent[li][None], gkr, g_q_dsa[li][None], g_k_dsa[li][None])
            w_out = w_out_even[li].astype(BF16)
            z = rms_matmul(yp, ln_mix_even[li][None], w, TM, E_N // 2)
            qm, lat, kr, lk, qd, kd, vd, qi, ki, wi = even_post(z, cos_p, sin_p, *gains, 256, BF16, S // 256)
            kv = matmul(lk, wkv, BF16, TM, wkv.shape[1] // 3)
            r3 = lambda a: a.reshape(B, S, a.shape[1])
            RM = MLA_H // 2
            o_mla = flash_attention(r3(qm), r3(kv), r3(kv), G=2, R=RM, dq=MLA_PAD, dv=MLA_V, T=TA, TS=TS,
                                    shared_kv=False, v_block0=MLA_H * MLA_PAD // (RM * MLA_V))
            wi_t = r3(wi)[:, :, :8].transpose(0, 2, 1)
            mask = dsa_prompt_mask_t(r3(qi), wi_t, r3(ki), 128, TA)
            o_dsa = flash_attention(r3(qd), r3(kd), r3(vd), G=DSA_G, R=DSA_H // DSA_G, dq=DSA_D, dv=DSA_D, T=TA, TS=TS,
                                    mask=mask)
            yp = out_proj(yp, o_mla.reshape(MP, -1), o_dsa.reshape(MP, -1), w_out, TM, D)
            outs.setdefault('lat_p', []).append(lat.reshape(B, S, MLA_LAT))
            outs.setdefault('kr_p', []).append(kr.reshape(B, S, MLA_ROPE))
            outs.setdefault('dk_p', []).append(kd.reshape(B, S, DSA_G, DSA_D))
            outs.setdefault('dv_p', []).append(vd.reshape(B, S, DSA_G, DSA_D))
            outs.setdefault('di_p', []).append(ki.reshape(B, S, IDX_D))
            z = rms_matmul(ys, ln_mix_even[li][None], w, N, E_N // 2)
            qm, lat, kr, lk, qd, kd, vd, qi, ki, wi = even_post(z, cos_s, sin_s, *gains, N, F32, 1)
            qm3 = qm.reshape(N, MLA_H, MLA_PAD)
            q_nope = qm3[:, :, :MLA_NOPE].reshape(N, MLA_H * MLA_NOPE)
            q_rope = qm3[:, :, MLA_NOPE:MLA_QK]
            w_uk2 = w_uk[li].reshape(MLA_LAT, MLA_H * MLA_NOPE)
            q_lat = per_head_matmul(q_nope, w_uk2, MLA_NOPE, MLA_LAT, True).reshape(N, MLA_H, MLA_LAT)
            o_lat = mla_decode(page_table, q_lat, q_rope, lat[:, None, :], kr[:, None, :],
                               cache_mla_latent, jnp.swapaxes(cache_mla_krope, 2, 3), li, PPC)
            o_mla = per_head_matmul(o_lat.reshape(N, MLA_H * MLA_LAT), w_uv[li].reshape(MLA_LAT, MLA_H * MLA_V),
                                    MLA_LAT, MLA_V, False)
            sc = dsa_sample_scores(page_table, qi.reshape(N, IDX_H, IDX_D).transpose(0, 2, 1), wi[:, :IDX_H, None],
                                   jnp.swapaxes(cache_dsa_idx_k, 2, 3), li, PPC)
            bias_p, bias_n, page_used = dsa_sample_select(sc.reshape(N, P), qi, wi, ki, 32)
            o_dsa = gqa_decode(page_table, qd.reshape(N, DSA_H, DSA_D), kd.reshape(N, DSA_G, DSA_D),
                               vd.reshape(N, DSA_G, DSA_D), bias_p[:, None, :], bias_n[:, None, :],
                               cache_dsa_k, cache_dsa_v, li, PPC, 'mask', page_used)
            ys = out_proj(ys, o_mla, o_dsa.reshape(N, -1), w_out, N, 1024)
            outs.setdefault('lat_s', []).append(lat.reshape(N, 1, MLA_LAT))
            outs.setdefault('kr_s', []).append(kr.reshape(N, 1, MLA_ROPE))
            outs.setdefault('dk_s', []).append(kd.reshape(N, 1, DSA_G, DSA_D))
            outs.setdefault('dv_s', []).append(vd.reshape(N, 1, DSA_G, DSA_D))
            outs.setdefault('di_s', []).append(ki.reshape(N, 1, IDX_D))
        else:
            w, bf = _odd_weights(w_in_odd[li], fox_b_f[li])
            s5p = (s5_lam_re[li], s5_lam_im[li], s5_log_dt[li], s5_b_re[li], s5_b_im[li], s5_c_re[li], s5_c_im[li])
            w_glu = s5_w_glu[li].astype(BF16)
            w_out = w_out_odd[li].astype(BF16)
            R = FOX_H // FOX_G
            z = rms_matmul(yp, ln_mix_odd[li][None], w, TM, O_N // 3)
            u, q, k, v, lf, cum = odd_post(z, g_q_fox[li][None], g_k_fox[li][None], bf, 256, BF16, B, True)
            y5, sre, sim = s5_prompt(u.reshape(B, S, S5_CH), *s5p)
            o_s5 = s5_glu(y5, u, s5_d[li][None], w_glu, s5_b_glu[li][None], TM)
            cum3 = cum[:, :FOX_H].reshape(B, S, FOX_G, R)
            cq = jnp.pad(cum3.transpose(0, 2, 1, 3), ((0, 0), (0, 0), (0, 0), (0, 128 - R)))
            ck = jnp.pad(cum3.transpose(0, 2, 3, 1), ((0, 0), (0, 0), (0, 8 - R), (0, 0)))
            r3 = lambda a: a.reshape(B, S, a.shape[1])
            o_fox = flash_attention(r3(q), r3(k), r3(v), G=FOX_G, R=R, dq=FOX_D, dv=FOX_D, T=TA, TS=TS, cq=cq, ck=ck)
            yp = out_proj(yp, o_s5, o_fox.reshape(MP, -1), w_out, TM, D)
            outs.setdefault('fk_p', []).append(k.reshape(B, S, FOX_G, FOX_D))
            outs.setdefault('fv_p', []).append(v.reshape(B, S, FOX_G, FOX_D))
            outs.setdefault('fl_p', []).append(lf[:, :FOX_H].reshape(B, S, FOX_H))
            outs.setdefault('sre_p', []).append(sre)
            outs.setdefault('sim_p', []).append(sim)
            z = rms_matmul(ys, ln_mix_odd[li][None], w, N, O_N // 3)
            u, q, k, v, lf, _ = odd_post(z, g_q_fox[li][None], g_k_fox[li][None], bf, N, F32, 1, False)
            y5, sre, sim = s5_sample(u, state_s5_re[li], state_s5_im[li], *s5p)
            o_s5 = s5_glu(y5, u, s5_d[li][None], w_glu, s5_b_glu[li][None], N)
            sfx = fox_suffix(page_table, jnp.swapaxes(cache_fox_logf, 2, 3), li)
            o_fox = gqa_decode(page_table, q.reshape(N, FOX_H, FOX_D), k.reshape(N, FOX_G, FOX_D),
                               v.reshape(N, FOX_G, FOX_D), sfx, lf[:, :FOX_H, None],
                               cache_fox_k, cache_fox_v, li, PPC, 'forget')
            ys = out_proj(ys, o_s5, o_fox.reshape(N, -1), w_out, N, 1024)
            outs.setdefault('fk_s', []).append(k.reshape(N, 1, FOX_G, FOX_D))
            outs.setdefault('fv_s', []).append(v.reshape(N, 1, FOX_G, FOX_D))
            outs.setdefault('fl_s', []).append(lf[:, :FOX_H].reshape(N, 1, FOX_H))
            outs.setdefault('sre_s', []).append(sre)
            outs.setdefault('sim_s', []).append(sim)
        wu = w_up[layer].astype(BF16)
        wd = w_down[layer].astype(BF16)
        yp = mlp(yp, ln_mlp[layer][None], wu, wd, TM, 512)
        ys = mlp(ys, ln_mlp[layer][None], wu, wd, N, 512)
    order = ('lat_p', 'lat_s', 'kr_p', 'kr_s', 'dk_p', 'dk_s', 'dv_p', 'dv_s', 'di_p', 'di_s',
             'fk_p', 'fk_s', 'fv_p', 'fv_s', 'fl_p', 'fl_s', 'sre_p', 'sre_s', 'sim_p', 'sim_s')
    return (yp.reshape(B, S, D), ys.reshape(N, 1, D)) + tuple(jnp.stack(outs[name]) for name in order)
```
